```python
import jax, jax.numpy as jnp
from jax import lax
import numpy as np

D_MODEL = 4096
BATCH = 4
SEQ = 2048
DEPTH = 2
DEC_BATCH = 8
DEC_SEQ = 32
PAST_LEN = 4096

CHUNK = 64
D_PLE = 256
EPS = 1e-6
TINY = 1e-30
GLA_H = 4
GLA_DK = 128
GLA_DV = 256
GLA_RANK = 16
GLA_GATE_NORM = 16.0
GDN_H = 16
GDN_DK = 128
GDN_DV = 128
GDN_CONV = 4
HG_H = 8
HG_DK = 128
HG_DV = 128
GLA_KW = GLA_H * GLA_DK
GLA_VW = GLA_H * GLA_DV
GDN_KW = GDN_H * GDN_DK
GDN_VW = GDN_H * GDN_DV
GDN_QKV = 2 * GDN_KW + GDN_VW
HG_KW = HG_H * HG_DK
HG_VW = HG_H * HG_DV
D_MIX = GLA_VW + GDN_VW + HG_VW
D_FF = 11008
FFN_CONV = 3
IN_SPLITS = (GLA_KW, GLA_KW, GLA_VW, GLA_VW, GLA_RANK,
             GDN_QKV, GDN_VW, GDN_H, GDN_H,
             HG_KW, HG_KW, HG_VW, HG_VW)
N_IN = sum(IN_SPLITS)

kernel_name = "hybrid_gla_gdn_hgrn2_stream_step"


def rms_norm(x, w):
    x32 = x.astype(jnp.float32)
    y = x32 * lax.rsqrt(jnp.mean(x32 * x32, axis=-1, keepdims=True) + EPS)
    return (y * w.astype(jnp.float32)).astype(x.dtype)


def l2_norm(x):
    return x * lax.rsqrt(jnp.sum(x * x, axis=-1, keepdims=True) + EPS)


def split_cols(a, sizes):
    out, start = [], 0
    for s in sizes:
        out.append(a[..., start:start + s])
        start += s
    return out


def heads(a, n_heads):
    b, t, _ = a.shape
    return jnp.moveaxis(a.reshape(b, t, n_heads, -1), 2, 1)


def to_chunks(a, chunk):
    b, h, t = a.shape[:3]
    return jnp.moveaxis(a.reshape(b, h, t // chunk, chunk, *a.shape[3:]), 2, 0)


def from_chunks(a):
    n, b, h, c, d = a.shape
    return jnp.moveaxis(a, 0, 2).reshape(b, h, n * c, d)


def masked_exp(mask, diff):
    return jnp.where(mask, jnp.exp(jnp.where(mask, diff, 0.0)), 0.0)


def causal_dwconv(x, buf, w):
    width = w.shape[0]
    t = x.shape[1]
    xp = jnp.concatenate([buf.astype(x.dtype), x], axis=1)
    wx = w.astype(x.dtype)
    y = xp[:, 0:t] * wx[0]
    for j in range(1, width):
        y = y + xp[:, j:j + t] * wx[j]
    return y, xp[:, -(width - 1):]


def gated_head_norm(o, gate, w):
    b, h, t, d = o.shape
    o = o * lax.rsqrt(jnp.mean(o * o, axis=-1, keepdims=True) + EPS) * w.astype(jnp.float32)
    o = jnp.moveaxis(o, 1, 2).reshape(b, t, h * d)
    return (o * jax.nn.silu(gate.astype(jnp.float32))).astype(gate.dtype)


def gated_linear_scan(q, k, v, log_f, state, chunk):
    causal = jnp.tril(jnp.ones((chunk, chunk), dtype=bool))[:, :, None]

    def step(s, inp):
        qc, kc, vc, gc = inp
        b = jnp.cumsum(gc, axis=2)
        decay = masked_exp(causal, b[:, :, :, None, :] - b[:, :, None, :, :])
        attn = jnp.einsum('bhtd,bhsd,bhtsd->bhts', qc, kc, decay)
        o = (jnp.einsum('bhtd,bhde->bhte', qc * jnp.exp(b), s)
             + jnp.einsum('bhts,bhse->bhte', attn, vc))
        b_last = b[:, :, -1, :]
        s = (jnp.exp(b_last)[..., None] * s
             + jnp.einsum('bhsd,bhse->bhde', kc * jnp.exp(b_last[:, :, None, :] - b), vc))
        return s, o

    s, o = lax.scan(step, state, (to_chunks(q, chunk), to_chunks(k, chunk),
                                  to_chunks(v, chunk), to_chunks(log_f, chunk)))
    return from_chunks(o), s


def gated_delta_scan(q, k, v, beta, log_a, state, chunk):
    dv = v.shape[-1]
    causal = jnp.tril(jnp.ones((chunk, chunk), dtype=bool))
    eye = jnp.eye(chunk, dtype=bool)
    strict = causal & ~eye
    eye_f = eye.astype(jnp.float32)

    def step(s, inp):
        qc, kc, vc, bc, gc = inp
        b = jnp.cumsum(gc, axis=-1)
        decay = masked_exp(causal, b[..., :, None] - b[..., None, :])
        kk = jnp.einsum('bhtd,bhsd->bhts', kc, kc)
        a_mat = jnp.where(strict, bc[..., :, None] * kk * decay, 0.0) + eye_f
        rhs = jnp.concatenate([vc * bc[..., None], kc * (bc * jnp.exp(b))[..., None]], axis=-1)
        sol = lax.linalg.triangular_solve(a_mat, rhs, left_side=True, lower=True,
                                          unit_diagonal=True)
        u, w = sol[..., :dv], sol[..., dv:]
        v_new = u - jnp.einsum('bhtd,bhde->bhte', w, s)
        qk = jnp.einsum('bhtd,bhsd->bhts', qc, kc) * decay
        o = (jnp.einsum('bhtd,bhde->bhte', qc * jnp.exp(b)[..., None], s)
             + jnp.einsum('bhts,bhse->bhte', qk, v_new))
        b_last = b[..., -1]
        s = (jnp.exp(b_last)[..., None, None] * s
             + jnp.einsum('bhsd,bhse->bhde', kc * jnp.exp(b_last[..., None] - b)[..., None], v_new))
        return s, o

    s, o = lax.scan(step, state, (to_chunks(q, chunk), to_chunks(k, chunk), to_chunks(v, chunk),
                                  to_chunks(beta, chunk), to_chunks(log_a, chunk)))
    return from_chunks(o), s


def gla_mixer(q, k, v, g, lr, w_gate, b_gate, norm_w, state, chunk):
    f32 = jnp.float32
    log_f = jax.nn.log_sigmoid((lr @ w_gate + b_gate).astype(f32)) / GLA_GATE_NORM
    qh = heads(q.astype(f32), GLA_H) * (GLA_DK ** -0.5)
    o, s = gated_linear_scan(qh, heads(k.astype(f32), GLA_H), heads(v.astype(f32), GLA_H),
                             heads(log_f, GLA_H), state.astype(f32), chunk)
    return gated_head_norm(o, g, norm_w), s


def gdn_mixer(qkv, z, b_raw, a_raw, conv_w, a_log, dt_bias, norm_w, state, conv_buf, chunk):
    f32 = jnp.float32
    qkv, new_buf = causal_dwconv(qkv, conv_buf, conv_w)
    qkv = jax.nn.silu(qkv.astype(f32))
    q, k, v = split_cols(qkv, (GDN_KW, GDN_KW, GDN_VW))
    q = l2_norm(heads(q, GDN_H)) * (GDN_DK ** -0.5)
    k = l2_norm(heads(k, GDN_H))
    v = heads(v, GDN_H)
    beta = jnp.moveaxis(jax.nn.sigmoid(b_raw.astype(f32)), 2, 1)
    log_a = jnp.moveaxis(-jnp.exp(a_log.astype(f32))
                         * jax.nn.softplus(a_raw.astype(f32) + dt_bias.astype(f32)), 2, 1)
    o, s = gated_delta_scan(q, k, v, beta, log_a, state.astype(f32), chunk)
    return gated_head_norm(o, z, norm_w), s, new_buf


def hgrn_mixer(q, f, i, g, lb, norm_w, state, chunk):
    f32 = jnp.float32
    zf = f.astype(f32)
    log_lb = jnp.log(jnp.maximum(lb, TINY))
    log_f = jnp.logaddexp(log_lb, jnp.log1p(-lb) + jax.nn.log_sigmoid(zf))
    key = (1.0 - lb) * jax.nn.sigmoid(-zf)
    qh = heads(jax.nn.silu(q.astype(f32)), HG_H)
    o, s = gated_linear_scan(qh, heads(key, HG_H), heads(i.astype(f32), HG_H),
                             heads(log_f, HG_H), state.astype(f32), chunk)
    return gated_head_norm(o, g, norm_w), s


def trunk(x, pe, st_gla, st_gdn, cb_gdn, st_hg, cb_ffn, params, chunk):
    (norm_mix, w_in, w_gla_gate, b_gla_gate, gla_norm, w_gdn_conv, gdn_a_log, gdn_dt_bias,
     gdn_norm, hgrn_lb, hgrn_norm, w_out, norm_ffn, w_up, w_ffn_conv, w_down, norm_ple,
     w_ple_gate, w_ple_proj, norm_final) = params
    sm = jax.nn.softmax(hgrn_lb.astype(jnp.float32), axis=0)
    lower_bounds = jnp.cumsum(sm, axis=0) - sm[0]
    h = x
    n_gla, n_gdn, n_gconv, n_hg, n_fconv = [], [], [], [], []
    for li in range(DEPTH):
        xn = rms_norm(h, norm_mix[li])
        (gq, gk, gv, gg, glr, dqkv, dz, db, da, hq, hf, hi, hg) = split_cols(xn @ w_in[li], IN_SPLITS)
        o_gla, s_gla = gla_mixer(gq, gk, gv, gg, glr, w_gla_gate[li], b_gla_gate[li],
                                 gla_norm[li], st_gla[li], chunk)
        o_gdn, s_gdn, b_gdn = gdn_mixer(dqkv, dz, db, da, w_gdn_conv[li], gdn_a_log[li],
                                        gdn_dt_bias[li], gdn_norm[li], st_gdn[li], cb_gdn[li], chunk)
        o_hg, s_hg = hgrn_mixer(hq, hf, hi, hg, lower_bounds[li], hgrn_norm[li], st_hg[li], chunk)
        h = h + jnp.concatenate([o_gla, o_gdn, o_hg], axis=-1) @ w_out[li]
        xn = rms_norm(h, norm_ffn[li])
        up, b_ffn = causal_dwconv(xn @ w_up[li], cb_ffn[li], w_ffn_conv[li])
        gate, val = split_cols(up, (D_FF, D_FF))
        h = h + (jax.nn.silu(gate) * val) @ w_down[li]
        ple_gate = jax.nn.sigmoid(rms_norm(h, norm_ple[li]) @ w_ple_gate[li])
        h = h + ple_gate * (pe[li].astype(h.dtype) @ w_ple_proj[li])
        n_gla.append(s_gla)
        n_gdn.append(s_gdn)
        n_gconv.append(b_gdn)
        n_hg.append(s_hg)
        n_fconv.append(b_ffn)
    y = rms_norm(h, norm_final)
    return (y, jnp.stack(n_gla), jnp.stack(n_gdn), jnp.stack(n_gconv),
            jnp.stack(n_hg), jnp.stack(n_fconv))


def setup_inputs(seed: int = 0) -> dict:
    key = jax.random.key(seed)
    keys = jax.random.split(key, 32)
    counter = [0]

    def nxt():
        k = keys[counter[0]]
        counter[0] += 1
        return k

    def nrm(shape, scale):
        return jax.random.normal(nxt(), shape, jnp.float32) * scale

    def gain(shape):
        return 1.0 + nrm(shape, 0.05)

    return {
        'x_prompt': nrm((BATCH, SEQ, D_MODEL), 1.0),
        'x_sample': nrm((DEC_BATCH, DEC_SEQ, D_MODEL), 1.0),
        'p_prompt': nrm((DEPTH, BATCH, SEQ, D_PLE), 1.0),
        'p_sample': nrm((DEPTH, DEC_BATCH, DEC_SEQ, D_PLE), 1.0),
        'state_gla': nrm((DEPTH, DEC_BATCH, GLA_H, GLA_DK, GLA_DV), 0.5),
        'state_gdn': nrm((DEPTH, DEC_BATCH, GDN_H, GDN_DK, GDN_DV), 0.1),
        'cache_gdn_conv': nrm((DEPTH, DEC_BATCH, GDN_CONV - 1, GDN_QKV), 1.0),
        'state_hgrn': nrm((DEPTH, DEC_BATCH, HG_H, HG_DK, HG_DV), 0.5),
        'cache_ffn_conv': nrm((DEPTH, DEC_BATCH, FFN_CONV - 1, 2 * D_FF), 1.0),
        'norm_mix': gain((DEPTH, D_MODEL)),
        'w_in': nrm((DEPTH, D_MODEL, N_IN), D_MODEL ** -0.5),
        'w_gla_gate': nrm((DEPTH, GLA_RANK, GLA_KW), GLA_RANK ** -0.5),
        'b_gla_gate': nrm((DEPTH, GLA_KW), 0.1),
        'gla_norm': gain((DEPTH, GLA_DV)),
        'w_gdn_conv': nrm((DEPTH, GDN_CONV, GDN_QKV), GDN_CONV ** -0.5),
        'gdn_a_log': jnp.log(jax.random.uniform(nxt(), (DEPTH, GDN_H), jnp.float32, 1.0, 16.0)),
        'gdn_dt_bias': nrm((DEPTH, GDN_H), 0.1),
        'gdn_norm': gain((DEPTH, GDN_DV)),
        'hgrn_lb': nrm((DEPTH, HG_KW), 1.0),
        'hgrn_norm': gain((DEPTH, HG_DV)),
        'w_out': nrm((DEPTH, D_MIX, D_MODEL), D_MIX ** -0.5),
        'norm_ffn': gain((DEPTH, D_MODEL)),
        'w_up': nrm((DEPTH, D_MODEL, 2 * D_FF), D_MODEL ** -0.5),
        'w_ffn_conv': nrm((DEPTH, FFN_CONV, 2 * D_FF), FFN_CONV ** -0.5),
        'w_down': nrm((DEPTH, D_FF, D_MODEL), D_FF ** -0.5),
        'norm_ple': gain((DEPTH, D_MODEL)),
        'w_ple_gate': nrm((DEPTH, D_MODEL, D_MODEL), D_MODEL ** -0.5),
        'w_ple_proj': nrm((DEPTH, D_PLE, D_MODEL), D_PLE ** -0.5),
        'norm_final': gain((D_MODEL,)),
    }


def reference(x_prompt, x_sample, p_prompt, p_sample, state_gla, state_gdn, cache_gdn_conv,
              state_hgrn, cache_ffn_conv, norm_mix, w_in, w_gla_gate, b_gla_gate, gla_norm,
              w_gdn_conv, gdn_a_log, gdn_dt_bias, gdn_norm, hgrn_lb, hgrn_norm, w_out, norm_ffn,
              w_up, w_ffn_conv, w_down, norm_ple, w_ple_gate, w_ple_proj, norm_final):
    params = (norm_mix, w_in, w_gla_gate, b_gla_gate, gla_norm, w_gdn_conv, gdn_a_log,
              gdn_dt_bias, gdn_norm, hgrn_lb, hgrn_norm, w_out, norm_ffn, w_up, w_ffn_conv,
              w_down, norm_ple, w_ple_gate, w_ple_proj, norm_final)
    f32 = jnp.float32
    bp, tp = x_prompt.shape[0], x_prompt.shape[1]
    zero_gla = jnp.zeros((DEPTH, bp, GLA_H, GLA_DK, GLA_DV), f32)
    zero_gdn = jnp.zeros((DEPTH, bp, GDN_H, GDN_DK, GDN_DV), f32)
    zero_gconv = jnp.zeros((DEPTH, bp, GDN_CONV - 1, GDN_QKV), x_prompt.dtype)
    zero_hg = jnp.zeros((DEPTH, bp, HG_H, HG_DK, HG_DV), f32)
    zero_fconv = jnp.zeros((DEPTH, bp, FFN_CONV - 1, 2 * D_FF), x_prompt.dtype)
    (y_prompt, p_state_gla, p_state_gdn, p_cache_gdn_conv, p_state_hgrn,
     p_cache_ffn_conv) = trunk(x_prompt, p_prompt, zero_gla, zero_gdn, zero_gconv, zero_hg,
                               zero_fconv, params, min(CHUNK, tp))
    (y_sample, s_state_gla, s_state_gdn, s_cache_gdn_conv, s_state_hgrn,
     s_cache_ffn_conv) = trunk(x_sample, p_sample, state_gla, state_gdn, cache_gdn_conv,
                               state_hgrn, cache_ffn_conv, params, x_sample.shape[1])
    return (y_prompt, y_sample, p_state_gla, p_state_gdn, p_cache_gdn_conv, p_state_hgrn,
            p_cache_ffn_conv, s_state_gla, s_state_gdn, s_cache_gdn_conv, s_state_hgrn,
            s_cache_ffn_conv)
```

```python
import functools

import numpy as np
import jax
import jax.numpy as jnp
from jax import lax
from jax.experimental import pallas as pl
from jax.experimental.pallas import tpu as pltpu

F32 = jnp.float32
BF16 = jnp.bfloat16

EPS = 1e-6
TINY = 1e-30
GLA_GATE_NORM = 16.0
GLA_H, GLA_DK, GLA_DV, GLA_RANK = 4, 128, 256, 16
GDN_H, GDN_DK, GDN_DV, GDN_CONV = 16, 128, 128, 4
HG_H, HG_DK, HG_DV = 8, 128, 128
FFN_CONV = 3
MAX_CHUNK = 64

LANE = 128
VMEM_LIMIT = 56 * 1024 * 1024


def _cparams(sem):
    return pltpu.CompilerParams(dimension_semantics=sem, vmem_limit_bytes=VMEM_LIMIT)


def _dot(a, b):
    return jnp.dot(a, b, preferred_element_type=F32)


def _dot_nt(a, b):
    return lax.dot_general(a, b, (((1,), (1,)), ((), ())), preferred_element_type=F32)


def _dot_tn(a, b):
    return lax.dot_general(a, b, (((0,), (0,)), ((), ())), preferred_element_type=F32)


def _mm(a, b):
    return _dot(a.astype(BF16), b.astype(BF16))


def _split3(x):
    hi = x.astype(BF16)
    r = x - hi.astype(F32)
    mid = r.astype(BF16)
    lo = (r - mid.astype(F32)).astype(BF16)
    return hi, mid, lo


def _sel_dot(p, x):
    hi, mid, lo = _split3(x)
    return _dot(p, hi) + _dot(p, mid) + _dot(p, lo)


def _dot_sel(x, p):
    hi, mid, lo = _split3(x)
    return _dot(hi, p) + _dot(mid, p) + _dot(lo, p)


def _softplus(x):
    return jnp.maximum(x, 0.0) + jnp.log1p(jnp.exp(-jnp.abs(x)))


def _log_sigmoid(x):
    return -_softplus(-x)


def _silu(x):
    return x * jax.nn.sigmoid(x)


def _rmsnorm_kernel(x_ref, w_ref, o_ref):
    x = x_ref[...]
    y = x * lax.rsqrt(jnp.mean(x * x, axis=-1, keepdims=True) + EPS)
    o_ref[...] = (y * w_ref[...]).astype(o_ref.dtype)


def rmsnorm(x, w, out_dtype, tm):
    m, d = x.shape
    return pl.pallas_call(
        _rmsnorm_kernel,
        grid=(m // tm,),
        in_specs=[pl.BlockSpec((tm, d), lambda i: (i, 0)),
                  pl.BlockSpec((1, d), lambda i: (0, 0))],
        out_specs=pl.BlockSpec((tm, d), lambda i: (i, 0)),
        out_shape=jax.ShapeDtypeStruct((m, d), out_dtype),
        compiler_params=_cparams(("parallel",)),
        name="rmsnorm",
    )(x, w.reshape(1, d))


def _mm_kernel(x_ref, w_ref, o_ref):
    o_ref[...] = _dot(x_ref[...], w_ref[...]).astype(o_ref.dtype)


def matmul(x, w, out_dtype, tm, tn):
    m, k = x.shape
    n = w.shape[1]
    return pl.pallas_call(
        _mm_kernel,
        grid=(m // tm, n // tn),
        in_specs=[pl.BlockSpec((tm, k), lambda i, j: (i, 0)),
                  pl.BlockSpec((k, tn), lambda i, j: (0, j))],
        out_specs=pl.BlockSpec((tm, tn), lambda i, j: (i, j)),
        out_shape=jax.ShapeDtypeStruct((m, n), out_dtype),
        compiler_params=_cparams(("parallel", "arbitrary")),
        name="matmul",
    )(x, w)


def _mm_res_kernel(x_ref, w_ref, r_ref, o_ref):
    k = pl.program_id(2)

    @pl.when(k == 0)
    def _():
        o_ref[...] = r_ref[...]

    o_ref[...] += _dot(x_ref[...], w_ref[...])


def matmul_residual(x, w, res, tm, tn, tk):
    m, k = x.shape
    n = w.shape[1]
    return pl.pallas_call(
        _mm_res_kernel,
        grid=(m // tm, n // tn, k // tk),
        in_specs=[pl.BlockSpec((tm, tk), lambda i, j, kk: (i, kk)),
                  pl.BlockSpec((tk, tn), lambda i, j, kk: (kk, j)),
                  pl.BlockSpec((tm, tn), lambda i, j, kk: (i, j))],
        out_specs=pl.BlockSpec((tm, tn), lambda i, j, kk: (i, j)),
        out_shape=jax.ShapeDtypeStruct((m, n), F32),
        compiler_params=_cparams(("parallel", "arbitrary", "arbitrary")),
        name="matmul_residual",
    )(x, w, res)


def _ple_kernel(x_ref, wg_ref, p_ref, wp_ref, r_ref, o_ref):
    gate = jax.nn.sigmoid(_dot(x_ref[...], wg_ref[...]))
    o_ref[...] = r_ref[...] + gate * _dot(p_ref[...], wp_ref[...])


def ple_update(xn, wg, pe, wp, res, tm, tn):
    m, k = xn.shape
    n = wg.shape[1]
    kp = pe.shape[1]
    return pl.pallas_call(
        _ple_kernel,
        grid=(m // tm, n // tn),
        in_specs=[pl.BlockSpec((tm, k), lambda i, j: (i, 0)),
                  pl.BlockSpec((k, tn), lambda i, j: (0, j)),
                  pl.BlockSpec((tm, kp), lambda i, j: (i, 0)),
                  pl.BlockSpec((kp, tn), lambda i, j: (0, j)),
                  pl.BlockSpec((tm, tn), lambda i, j: (i, j))],
        out_specs=pl.BlockSpec((tm, tn), lambda i, j: (i, j)),
        out_shape=jax.ShapeDtypeStruct((m, n), F32),
        compiler_params=_cparams(("parallel", "arbitrary")),
        name="ple_update",
    )(xn, wg, pe, wp, res)


def _ffn_act_kernel(g_ref, v_ref, wg_ref, wv_ref, cg0_ref, cv0_ref,
                    h_ref, cg_ref, cv_ref, sg, sv, *, tr, n_row_tiles):
    i = pl.program_id(2)
    last = FFN_CONV - 1

    def conv(x_ref, w_ref, c0_ref, c_ref, scr):
        @pl.when(i == 0)
        def _():
            scr[8 - last:8, :] = c0_ref[0]

        scr[8:8 + tr, :] = x_ref[...]
        y = scr[8:8 + tr, :] * w_ref[last:last + 1, :]
        for j in range(last):
            y = y + scr[8 - last + j:8 - last + j + tr, :] * w_ref[j:j + 1, :]
        tail = scr[8 + tr - last:8 + tr, :]
        scr[8 - last:8, :] = tail

        @pl.when(i == n_row_tiles - 1)
        def _():
            c_ref[0] = tail

        return y

    gate = conv(g_ref, wg_ref, cg0_ref, cg_ref, sg)
    val = conv(v_ref, wv_ref, cv0_ref, cv_ref, sv)
    h_ref[...] = (_silu(gate) * val).astype(h_ref.dtype)


def ffn_conv_act(up, w_conv, cache, b, t, tr, tc):
    m, f2 = up.shape
    f = f2 // 2
    nj = f // tc
    n_row_tiles = t // tr
    last = FFN_CONV - 1
    kern = functools.partial(_ffn_act_kernel, tr=tr, n_row_tiles=n_row_tiles)
    hid, cg, cv = pl.pallas_call(
        kern,
        grid=(b, nj, n_row_tiles),
        in_specs=[pl.BlockSpec((tr, tc), lambda bb, j, i: (bb * n_row_tiles + i, j)),
                  pl.BlockSpec((tr, tc), lambda bb, j, i: (bb * n_row_tiles + i, nj + j)),
                  pl.BlockSpec((FFN_CONV, tc), lambda bb, j, i: (0, j)),
                  pl.BlockSpec((FFN_CONV, tc), lambda bb, j, i: (0, nj + j)),
                  pl.BlockSpec((1, last, tc), lambda bb, j, i: (bb, 0, j)),
                  pl.BlockSpec((1, last, tc), lambda bb, j, i: (bb, 0, nj + j))],
        out_specs=[pl.BlockSpec((tr, tc), lambda bb, j, i: (bb * n_row_tiles + i, j)),
                   pl.BlockSpec((1, last, tc), lambda bb, j, i: (bb, 0, j)),
                   pl.BlockSpec((1, last, tc), lambda bb, j, i: (bb, 0, j))],
        out_shape=[jax.ShapeDtypeStruct((m, f), BF16),
                   jax.ShapeDtypeStruct((b, last, f), F32),
                   jax.ShapeDtypeStruct((b, last, f), F32)],
        scratch_shapes=[pltpu.VMEM((8 + tr, tc), F32), pltpu.VMEM((8 + tr, tc), F32)],
        compiler_params=_cparams(("parallel", "parallel", "arbitrary")),
        name="ffn_conv_act",
    )(up, up, w_conv, w_conv, cache, cache)
    return hid, jnp.concatenate([cg, cv], axis=-1)


def _levels(c):
    out, h = [], c // 2
    while h >= 1:
        out.append(h)
        h //= 2
    return out


def _prefix_matrix(c):
    blocks = [np.tril(np.ones((c, c), np.float32))]
    for h in _levels(c):
        p = np.zeros((c, c), np.float32)
        for t in range(c):
            mid = (t // (2 * h)) * 2 * h + h
            if t % (2 * h) >= h:
                p[t, mid:t + 1] = 1.0
            else:
                p[t, t + 1:mid] = 1.0
        blocks.append(p)
    return np.concatenate(blocks, axis=0)


def _gated_head_out(o, gate, w):
    o = o * lax.rsqrt(jnp.mean(o * o, axis=-1, keepdims=True) + EPS) * w
    return o * _silu(gate)


def _linear_attn_head(q, k, v, g, st, p_ref, c):
    z = _sel_dot(p_ref[...], g)
    b = z[0:c]
    row = lax.broadcasted_iota(jnp.int32, (c, 1), 0)
    ri = lax.broadcasted_iota(jnp.int32, (c, c), 0)
    ci = lax.broadcasted_iota(jnp.int32, (c, c), 1)
    attn = jnp.where(ri == ci, jnp.sum(q * k, axis=-1, keepdims=True), 0.0)
    for li, h in enumerate(_levels(c)):
        e = jnp.exp(z[(li + 1) * c:(li + 2) * c])
        lower = (row & (2 * h - 1)) >= h
        ql = jnp.where(lower, q * e, 0.0).astype(BF16)
        kl = jnp.where(lower, 0.0, k * e).astype(BF16)
        a = _dot_nt(ql, kl)
        if 2 * h < c:
            sh = (2 * h).bit_length() - 1
            a = jnp.where((ri >> sh) == (ci >> sh), a, 0.0)
        attn = attn + a
    stb = st.astype(BF16)
    o = _dot_nt((q * jnp.exp(b)).astype(BF16), stb) + _mm(attn, v)
    b_last = b[c - 1:c]
    kt = (k * jnp.exp(b_last - b)).astype(BF16)
    st_new = jnp.exp(b_last) * st + _dot_tn(v.astype(BF16), kt)
    return o, st_new


def _gla_kernel(q_ref, k_ref, v_ref, gg_ref, sm_ref, wg_ref, bg_ref, nw_ref, p_ref, s0_ref,
                o_ref, s_ref, st, *, c, nc):
    ci = pl.program_id(2)

    @pl.when(ci == 0)
    def _():
        st[...] = s0_ref[0, 0]

    x = _dot_sel(sm_ref[...], wg_ref[...]) + bg_ref[...]
    g = _log_sigmoid(x) * (1.0 / GLA_GATE_NORM)
    q = q_ref[...] * (GLA_DK ** -0.5)
    o, st_new = _linear_attn_head(q, k_ref[...], v_ref[...], g, st[...], p_ref, c)
    st[...] = st_new
    o_ref[...] = _gated_head_out(o, gg_ref[...], nw_ref[...]).astype(o_ref.dtype)

    @pl.when(ci == nc - 1)
    def _():
        s_ref[0, 0] = st_new


def gla_mixer(proj, small, wg_pad, bg, norm_w, state_t, b, t, c):
    m = proj.shape[0]
    nc = t // c
    p = jnp.asarray(_prefix_matrix(c), BF16)
    row = lambda bb, h, cc: bb * nc + cc
    kq = GLA_H
    kern = functools.partial(_gla_kernel, c=c, nc=nc)
    return pl.pallas_call(
        kern,
        grid=(b, GLA_H, nc),
        in_specs=[pl.BlockSpec((c, GLA_DK), lambda bb, h, cc: (row(bb, h, cc), h)),
                  pl.BlockSpec((c, GLA_DK), lambda bb, h, cc: (row(bb, h, cc), kq + h)),
                  pl.BlockSpec((c, GLA_DV), lambda bb, h, cc: (row(bb, h, cc), kq + h)),
                  pl.BlockSpec((c, GLA_DV), lambda bb, h, cc: (row(bb, h, cc), 2 * kq + h)),
                  pl.BlockSpec((c, LANE), lambda bb, h, cc: (row(bb, h, cc), 0)),
                  pl.BlockSpec((LANE, GLA_DK), lambda bb, h, cc: (0, h)),
                  pl.BlockSpec((1, GLA_DK), lambda bb, h, cc: (0, h)),
                  pl.BlockSpec((1, GLA_DV), lambda bb, h, cc: (0, 0)),
                  pl.BlockSpec(p.shape, lambda bb, h, cc: (0, 0)),
                  pl.BlockSpec((1, 1, GLA_DV, GLA_DK), lambda bb, h, cc: (bb, h, 0, 0))],
        out_specs=[pl.BlockSpec((c, GLA_DV), lambda bb, h, cc: (row(bb, h, cc), h)),
                   pl.BlockSpec((1, 1, GLA_DV, GLA_DK), lambda bb, h, cc: (bb, h, 0, 0))],
        out_shape=[jax.ShapeDtypeStruct((m, GLA_H * GLA_DV), BF16),
                   jax.ShapeDtypeStruct((b, GLA_H, GLA_DV, GLA_DK), F32)],
        scratch_shapes=[pltpu.VMEM((GLA_DV, GLA_DK), F32)],
        compiler_params=_cparams(("parallel", "parallel", "arbitrary")),
        name="gla_mixer",
    )(proj, proj, proj, proj, small, wg_pad, bg.reshape(1, -1), norm_w.reshape(1, -1), p, state_t)


HG_HB = 4


def _hgrn_kernel(q_ref, f_ref, i_ref, g_ref, lb_ref, nw_ref, p_ref, s0_ref,
                 o_ref, s_ref, st, *, c, nc):
    ci = pl.program_id(2)

    @pl.when(ci == 0)
    def _():
        st[...] = s0_ref[0]

    lb = lb_ref[...]
    log_lb = jnp.log(jnp.maximum(lb, TINY))
    zf = f_ref[...]
    a = log_lb
    cc = jnp.log1p(-lb) + _log_sigmoid(zf)
    log_f = jnp.maximum(a, cc) + jnp.log1p(jnp.exp(-jnp.abs(a - cc)))
    key = (1.0 - lb) * jax.nn.sigmoid(-zf)
    q = _silu(q_ref[...])
    v = i_ref[...]
    gate = g_ref[...]
    for hh in range(HG_HB):
        sl = slice(hh * HG_DK, (hh + 1) * HG_DK)
        o, st_new = _linear_attn_head(q[:, sl], key[:, sl], v[:, sl], log_f[:, sl], st[hh], p_ref, c)
        st[hh] = st_new
        o_ref[:, sl] = _gated_head_out(o, gate[:, sl], nw_ref[...]).astype(o_ref.dtype)

        @pl.when(ci == nc - 1)
        def _():
            s_ref[0, hh] = st_new


def hgrn_mixer(proj, lb, norm_w, state_t, b, t, c, col0):
    m = proj.shape[0]
    nc = t // c
    ng = HG_H // HG_HB
    w = HG_HB * HG_DK
    p = jnp.asarray(_prefix_matrix(c), BF16)
    row = lambda bb, g, cc: bb * nc + cc
    kern = functools.partial(_hgrn_kernel, c=c, nc=nc)
    return pl.pallas_call(
        kern,
        grid=(b, ng, nc),
        in_specs=[pl.BlockSpec((c, w), lambda bb, g, cc: (row(bb, g, cc), col0 + g)),
                  pl.BlockSpec((c, w), lambda bb, g, cc: (row(bb, g, cc), col0 + ng + g)),
                  pl.BlockSpec((c, w), lambda bb, g, cc: (row(bb, g, cc), col0 + 2 * ng + g)),
                  pl.BlockSpec((c, w), lambda bb, g, cc: (row(bb, g, cc), col0 + 3 * ng + g)),
                  pl.BlockSpec((1, w), lambda bb, g, cc: (0, g)),
                  pl.BlockSpec((1, HG_DV), lambda bb, g, cc: (0, 0)),
                  pl.BlockSpec(p.shape, lambda bb, g, cc: (0, 0)),
                  pl.BlockSpec((1, HG_HB, HG_DV, HG_DK), lambda bb, g, cc: (bb, g, 0, 0))],
        out_specs=[pl.BlockSpec((c, w), lambda bb, g, cc: (row(bb, g, cc), g)),
                   pl.BlockSpec((1, HG_HB, HG_DV, HG_DK), lambda bb, g, cc: (bb, g, 0, 0))],
        out_shape=[jax.ShapeDtypeStruct((m, HG_H * HG_DV), BF16),
                   jax.ShapeDtypeStruct((b, HG_H, HG_DV, HG_DK), F32)],
        scratch_shapes=[pltpu.VMEM((HG_HB, HG_DV, HG_DK), F32)],
        compiler_params=_cparams(("parallel", "parallel", "arbitrary")),
        name="hgrn_mixer",
    )(proj, proj, proj, proj, lb.reshape(1, -1), norm_w.reshape(1, -1), p, state_t)


GDN_HB = 4


def _unit_lower_inverse(n, c):
    ri = lax.broadcasted_iota(jnp.int32, (c, c), 0)
    ci = lax.broadcasted_iota(jnp.int32, (c, c), 1)
    eye = (ri == ci).astype(F32)
    d = jnp.where((ri >> 4) == (ci >> 4), n, 0.0)
    x = eye - d
    d2 = _mm(d, d)
    x = x + _mm(x, d2)
    d4 = _mm(d2, d2)
    x = x + _mm(x, d4)
    d8 = _mm(d4, d4)
    x = x + _mm(x, d8)
    size = 16
    while size < c:
        sh = size.bit_length() - 1
        low = jnp.where(((ri >> (sh + 1)) == (ci >> (sh + 1))) & ((ri >> sh) > (ci >> sh)), n, 0.0)
        x = x - _mm(x, _mm(low, x))
        size *= 2
    return x


def _gdn_head(q, k, v, beta, b_c, b_r, s, c):
    ri = lax.broadcasted_iota(jnp.int32, (c, c), 0)
    ci = lax.broadcasted_iota(jnp.int32, (c, c), 1)
    decay = jnp.where(ri >= ci, jnp.exp(jnp.minimum(b_c - b_r, 0.0)), 0.0)
    kb = k.astype(BF16)
    qb = q.astype(BF16)
    kk = _dot_nt(kb, kb)
    n = jnp.where(ri > ci, beta * kk * decay, 0.0)
    tinv = _unit_lower_inverse(n, c)
    eb = jnp.exp(b_c)
    rhs = jnp.concatenate([v * beta, k * (beta * eb)], axis=-1)
    sol = _mm(tinv, rhs)
    u = sol[:, :GDN_DV]
    w = sol[:, GDN_DV:]
    sb = s.astype(BF16)
    v_new = u - _dot(w.astype(BF16), sb)
    vnb = v_new.astype(BF16)
    qk = _dot_nt(qb, kb) * decay
    o = _dot((q * eb).astype(BF16), sb) + _dot(qk.astype(BF16), vnb)
    b_last = b_c[c - 1:c]
    kt = (k * jnp.exp(b_last - b_c)).astype(BF16)
    s_new = jnp.exp(b_last) * s + _dot_tn(kt, vnb)
    return o, s_new


def _gdn_kernel(q_ref, k_ref, v_ref, z_ref, gc_ref, gr_ref, wq_ref, wk_ref, wv_ref,
                pc_ref, pr_ref, nw_ref, tril_ref, triu_ref, s0_ref, cq0_ref, ck0_ref, cv0_ref,
                o_ref, s_ref, cq_ref, ck_ref, cv_ref,
                st, xq, xk, xv, *, c, nc):
    ci = pl.program_id(2)
    last = GDN_CONV - 1

    @pl.when(ci == 0)
    def _():
        st[...] = s0_ref[0]

    def conv(x_ref, w_ref, c0_ref, c_ref, scr):
        @pl.when(ci == 0)
        def _():
            scr[8 - last:8, :] = c0_ref[0]

        scr[8:8 + c, :] = x_ref[...]
        y = scr[8:8 + c, :] * w_ref[last:last + 1, :]
        for j in range(last):
            y = y + scr[8 - last + j:8 - last + j + c, :] * w_ref[j:j + 1, :]
        tail = scr[8 + c - last:8 + c, :]
        scr[8 - last:8, :] = tail

        @pl.when(ci == nc - 1)
        def _():
            c_ref[0] = tail

        return _silu(y)

    qa = conv(q_ref, wq_ref, cq0_ref, cq_ref, xq)
    ka = conv(k_ref, wk_ref, ck0_ref, ck_ref, xk)
    va = conv(v_ref, wv_ref, cv0_ref, cv_ref, xv)

    gcol = gc_ref[0]
    grow = gr_ref[0, 0]
    pc = pc_ref[0]
    pr = pr_ref[0]
    beta_all = jax.nn.sigmoid(gcol)
    g_c = pc[0:1, :] * _softplus(gcol + pc[1:2, :])
    g_r = pr[:, 0:1] * _softplus(grow + pr[:, 1:2])
    b_col = _sel_dot(tril_ref[...], g_c)
    b_row = _dot_sel(g_r, triu_ref[...])
    z = z_ref[...]
    for hh in range(GDN_HB):
        sl = slice(hh * GDN_DK, (hh + 1) * GDN_DK)
        q = qa[:, sl]
        k = ka[:, sl]
        q = q * lax.rsqrt(jnp.sum(q * q, axis=-1, keepdims=True) + EPS) * (GDN_DK ** -0.5)
        k = k * lax.rsqrt(jnp.sum(k * k, axis=-1, keepdims=True) + EPS)
        beta = beta_all[:, hh:hh + 1]
        b_c = b_col[:, GDN_HB + hh:GDN_HB + hh + 1]
        b_r = b_row[GDN_HB + hh:GDN_HB + hh + 1, :]
        o, s_new = _gdn_head(q, k, va[:, sl], beta, b_c, b_r, st[hh], c)
        st[hh] = s_new
        o_ref[:, sl] = _gated_head_out(o, z[:, sl], nw_ref[...]).astype(o_ref.dtype)

        @pl.when(ci == nc - 1)
        def _():
            s_ref[0, hh] = s_new


def gdn_mixer(proj, gates_c, gates_r, conv_w, par_c, par_r, norm_w, state, cache, b, t, c, col0):
    m = proj.shape[0]
    nc = t // c
    ng = GDN_H // GDN_HB
    w = GDN_HB * GDN_DK
    last = GDN_CONV - 1
    tril = jnp.asarray(np.tril(np.ones((c, c), np.float32)), BF16)
    triu = jnp.asarray(np.triu(np.ones((c, c), np.float32)), BF16)
    row = lambda bb, g, cc: bb * nc + cc
    kern = functools.partial(_gdn_kernel, c=c, nc=nc)
    big = lambda off: pl.BlockSpec((c, w), lambda bb, g, cc: (row(bb, g, cc), col0 + off * ng + g))
    cw = lambda off: pl.BlockSpec((GDN_CONV, w), lambda bb, g, cc: (0, off * ng + g))
    c0 = lambda off: pl.BlockSpec((1, last, w), lambda bb, g, cc: (bb, 0, off * ng + g))
    cout = pl.BlockSpec((1, last, w), lambda bb, g, cc: (bb, 0, g))
    outs = pl.pallas_call(
        kern,
        grid=(b, ng, nc),
        in_specs=[big(0), big(1), big(2), big(3),
                  pl.BlockSpec((1, c, LANE), lambda bb, g, cc: (g, row(bb, g, cc), 0)),
                  pl.BlockSpec((1, 1, LANE, c), lambda bb, g, cc: (g, row(bb, g, cc), 0, 0)),
                  cw(0), cw(1), cw(2),
                  pl.BlockSpec((1, 2, LANE), lambda bb, g, cc: (g, 0, 0)),
                  pl.BlockSpec((1, LANE, 2), lambda bb, g, cc: (g, 0, 0)),
                  pl.BlockSpec((1, GDN_DV), lambda bb, g, cc: (0, 0)),
                  pl.BlockSpec((c, c), lambda bb, g, cc: (0, 0)),
                  pl.BlockSpec((c, c), lambda bb, g, cc: (0, 0)),
                  pl.BlockSpec((1, GDN_HB, GDN_DK, GDN_DV), lambda bb, g, cc: (bb, g, 0, 0)),
                  c0(0), c0(1), c0(2)],
        out_specs=[pl.BlockSpec((c, w), lambda bb, g, cc: (row(bb, g, cc), g)),
                   pl.BlockSpec((1, GDN_HB, GDN_DK, GDN_DV), lambda bb, g, cc: (bb, g, 0, 0)),
                   cout, cout, cout],
        out_shape=[jax.ShapeDtypeStruct((m, GDN_H * GDN_DV), BF16),
                   jax.ShapeDtypeStruct((b, GDN_H, GDN_DK, GDN_DV), F32),
                   jax.ShapeDtypeStruct((b, last, GDN_H * GDN_DK), F32),
                   jax.ShapeDtypeStruct((b, last, GDN_H * GDN_DK), F32),
                   jax.ShapeDtypeStruct((b, last, GDN_H * GDN_DV), F32)],
        scratch_shapes=[pltpu.VMEM((GDN_HB, GDN_DK, GDN_DV), F32),
                        pltpu.VMEM((8 + c, w), F32), pltpu.VMEM((8 + c, w), F32),
                        pltpu.VMEM((8 + c, w), F32)],
        compiler_params=_cparams(("parallel", "parallel", "arbitrary")),
        name="gdn_mixer",
    )(proj, proj, proj, proj, gates_c, gates_r, conv_w, conv_w, conv_w, par_c, par_r,
      norm_w.reshape(1, -1), tril, triu, state, cache, cache, cache)
    o, s, cq, ck, cv = outs
    return o, s, jnp.concatenate([cq, ck, cv], axis=-1)


def _row_tile(m, cap):
    tm = min(m, cap)
    while m % tm:
        tm //= 2
    return tm


def _group_gates(small, c):
    m = small.shape[0]
    ng = GDN_H // GDN_HB
    db = small[:, GLA_RANK:GLA_RANK + GDN_H].reshape(m, ng, GDN_HB)
    da = small[:, GLA_RANK + GDN_H:GLA_RANK + 2 * GDN_H].reshape(m, ng, GDN_HB)
    g = jnp.concatenate([db, da, jnp.zeros((m, ng, LANE - 2 * GDN_HB), F32)], axis=-1)
    gc = jnp.transpose(g, (1, 0, 2))
    gr = jnp.transpose(gc.reshape(ng, m // c, c, LANE), (0, 1, 3, 2))
    return gc, gr


def _group_params(a_log, dt_bias):
    ng = GDN_H // GDN_HB
    na = -jnp.exp(a_log.astype(F32)).reshape(ng, GDN_HB)
    dt = dt_bias.astype(F32).reshape(ng, GDN_HB)
    pad = jnp.zeros((ng, GDN_HB), F32)
    tail = jnp.zeros((ng, LANE - 2 * GDN_HB), F32)
    rows = jnp.stack([jnp.concatenate([pad, na, tail], axis=-1),
                      jnp.concatenate([pad, dt, tail], axis=-1)], axis=1)
    return rows, jnp.transpose(rows, (0, 2, 1))


def _prep_weights(w_in, w_gla_gate, w_out, w_up, w_down, w_ple_gate, w_ple_proj):
    d = w_in.shape[1]
    c0 = 2 * GLA_H * GLA_DK + 2 * GLA_H * GLA_DV
    c1 = c0 + GLA_RANK
    nb = 2 * GDN_H * GDN_DK + 2 * GDN_H * GDN_DV
    c2 = c1 + nb
    c3 = c2 + 2 * GDN_H
    w_main = jnp.concatenate([w_in[:, :, :c0], w_in[:, :, c1:c2], w_in[:, :, c3:]], axis=-1)
    w_small = jnp.concatenate([w_in[:, :, c0:c1], w_in[:, :, c2:c3]], axis=-1)
    w_small = jnp.pad(w_small, ((0, 0), (0, 0), (0, LANE - w_small.shape[-1])))
    wg_pad = jnp.pad(w_gla_gate, ((0, 0), (0, LANE - GLA_RANK), (0, 0)))
    cast = lambda a: a.astype(BF16)
    return (cast(w_main), cast(w_small), cast(wg_pad), cast(w_out), cast(w_up), cast(w_down),
            cast(w_ple_gate), cast(w_ple_proj))


def _trunk(x, pe, st_gla, st_gdn, cb_gdn, st_hg, cb_ffn, wts, prm, lower_bounds, norm_final):
    (w_main, w_small, wg_pad, w_out, w_up, w_down, w_pg, w_pp) = wts
    (norm_mix, b_gla_gate, gla_norm, w_gdn_conv, gdn_a_log, gdn_dt_bias, gdn_norm, hgrn_norm,
     norm_ffn, w_ffn_conv, norm_ple) = prm
    b, t, d = x.shape
    depth = w_main.shape[0]
    m = b * t
    c = min(MAX_CHUNK, t)
    tm = _row_tile(m, 1024)
    tr = _row_tile(t, 256)
    h = x.reshape(m, d)
    n_gla, n_gdn, n_gconv, n_hg, n_fconv = [], [], [], [], []
    for li in range(depth):
        xn = rmsnorm(h, norm_mix[li], BF16, _row_tile(m, 256))
        proj = matmul(xn, w_main[li], F32, tm, 512)
        small = matmul(xn, w_small[li], F32, tm, LANE)

        o_gla, s_gla = gla_mixer(proj, small, wg_pad[li], b_gla_gate[li], gla_norm[li],
                                 jnp.swapaxes(st_gla[li], -1, -2), b, t, c)
        gc, gr = _group_gates(small, c)
        pc, pr = _group_params(gdn_a_log[li], gdn_dt_bias[li])
        gdn_col0 = (2 * GLA_H * GLA_DK + 2 * GLA_H * GLA_DV) // (GDN_HB * GDN_DK)
        o_gdn, s_gdn, b_gdn = gdn_mixer(proj, gc, gr, w_gdn_conv[li], pc, pr, gdn_norm[li],
                                        st_gdn[li], cb_gdn[li], b, t, c, gdn_col0)
        hg_col0 = gdn_col0 + (2 * GDN_H * GDN_DK + 2 * GDN_H * GDN_DV) // (HG_HB * HG_DK)
        o_hg, s_hg = hgrn_mixer(proj, lower_bounds[li], hgrn_norm[li],
                                jnp.swapaxes(st_hg[li], -1, -2), b, t, c, hg_col0)
        mix = jnp.concatenate([o_gla, o_gdn, o_hg], axis=-1)
        h = matmul_residual(mix, w_out[li], h, tm, 512, mix.shape[1])

        xn = rmsnorm(h, norm_ffn[li], BF16, _row_tile(m, 256))
        up = matmul(xn, w_up[li], F32, tm, 512)
        f = w_down.shape[1]
        hid, b_ffn = ffn_conv_act(up, w_ffn_conv[li], cb_ffn[li], b, t, tr, f // 2)
        h = matmul_residual(hid, w_down[li], h, tm, 512, f // 2)

        xn = rmsnorm(h, norm_ple[li], BF16, _row_tile(m, 256))
        h = ple_update(xn, w_pg[li], pe[li].reshape(m, -1).astype(BF16), w_pp[li], h, tm, 512)

        n_gla.append(jnp.swapaxes(s_gla, -1, -2))
        n_gdn.append(s_gdn)
        n_gconv.append(b_gdn)
        n_hg.append(jnp.swapaxes(s_hg, -1, -2))
        n_fconv.append(b_ffn)
    y = rmsnorm(h, norm_final, F32, _row_tile(m, 256)).reshape(b, t, d)
    return (y, jnp.stack(n_gla), jnp.stack(n_gdn), jnp.stack(n_gconv),
            jnp.stack(n_hg), jnp.stack(n_fconv))


def kernel(x_prompt, x_sample, p_prompt, p_sample, state_gla, state_gdn, cache_gdn_conv, state_hgrn, cache_ffn_conv, norm_mix, w_in, w_gla_gate, b_gla_gate, gla_norm, w_gdn_conv, gdn_a_log, gdn_dt_bias, gdn_norm, hgrn_lb, hgrn_norm, w_out, norm_ffn, w_up, w_ffn_conv, w_down, norm_ple, w_ple_gate, w_ple_proj, norm_final):
    depth = w_in.shape[0]
    bp = x_prompt.shape[0]
    wts = _prep_weights(w_in, w_gla_gate, w_out, w_up, w_down, w_ple_gate, w_ple_proj)
    prm = (norm_mix, b_gla_gate, gla_norm, w_gdn_conv, gdn_a_log, gdn_dt_bias, gdn_norm,
           hgrn_norm, norm_ffn, w_ffn_conv, norm_ple)
    sm = jax.nn.softmax(hgrn_lb.astype(F32), axis=0)
    lower_bounds = jnp.cumsum(sm, axis=0) - sm[0]
    zeros = lambda a: jnp.zeros((depth, bp) + a.shape[2:], F32)
    prompt = _trunk(x_prompt, p_prompt, zeros(state_gla), zeros(state_gdn), zeros(cache_gdn_conv),
                    zeros(state_hgrn), zeros(cache_ffn_conv), wts, prm, lower_bounds, norm_final)
    sample = _trunk(x_sample, p_sample, state_gla, state_gdn, cache_gdn_conv, state_hgrn,
                    cache_ffn_conv, wts, prm, lower_bounds, norm_final)
    return (prompt[0], sample[0]) + prompt[1:] + sample[1:]
```

```python
import functools

import numpy as np
import jax
import jax.numpy as jnp
from jax import lax
from jax.experimental import pallas as pl
from jax.experimental.pallas import tpu as pltpu

F32 = jnp.float32
BF16 = jnp.bfloat16

EPS = 1e-6
TINY = 1e-30
GLA_GATE_NORM = 16.0
GLA_H, GLA_DK, GLA_DV, GLA_RANK = 4, 128, 256, 16
GDN_H, GDN_DK, GDN_DV, GDN_CONV = 16, 128, 128, 4
HG_H, HG_DK, HG_DV = 8, 128, 128
FFN_CONV = 3
MAX_CHUNK = 64
ROW_TILE = 1024

LANE = 128
VMEM_LIMIT = 56 * 1024 * 1024


def _cparams(sem):
    return pltpu.CompilerParams(dimension_semantics=sem, vmem_limit_bytes=VMEM_LIMIT)


def _dot(a, b):
    return jnp.dot(a, b, preferred_element_type=F32)


def _dot_nt(a, b):
    return lax.dot_general(a, b, (((1,), (1,)), ((), ())), preferred_element_type=F32)


def _dot_tn(a, b):
    return lax.dot_general(a, b, (((0,), (0,)), ((), ())), preferred_element_type=F32)


def _mm(a, b):
    return _dot(a.astype(BF16), b.astype(BF16))


def _split3(x):
    hi = x.astype(BF16)
    r = x - hi.astype(F32)
    mid = r.astype(BF16)
    lo = (r - mid.astype(F32)).astype(BF16)
    return hi, mid, lo


def _sel_dot(p, x):
    hi, mid, lo = _split3(x)
    return _dot(p, hi) + _dot(p, mid) + _dot(p, lo)


def _dot_sel(x, p):
    hi, mid, lo = _split3(x)
    return _dot(hi, p) + _dot(mid, p) + _dot(lo, p)


def _softplus(x):
    return jnp.maximum(x, 0.0) + jnp.log1p(jnp.exp(-jnp.abs(x)))


def _log_sigmoid(x):
    return -_softplus(-x)


def _silu(x):
    return x * jax.nn.sigmoid(x)


def _rmsnorm_kernel(x_ref, w_ref, o_ref):
    x = x_ref[...]
    y = x * lax.rsqrt(jnp.mean(x * x, axis=-1, keepdims=True) + EPS)
    o_ref[...] = (y * w_ref[...]).astype(o_ref.dtype)


def rmsnorm(x, w, out_dtype, tm):
    m, d = x.shape
    return pl.pallas_call(
        _rmsnorm_kernel,
        grid=(m // tm,),
        in_specs=[pl.BlockSpec((tm, d), lambda i: (i, 0)),
                  pl.BlockSpec((1, d), lambda i: (0, 0))],
        out_specs=pl.BlockSpec((tm, d), lambda i: (i, 0)),
        out_shape=jax.ShapeDtypeStruct((m, d), out_dtype),
        compiler_params=_cparams(("parallel",)),
        name="rmsnorm",
    )(x, w.reshape(1, d))


def _mm_kernel(x_ref, w_ref, o_ref):
    o_ref[...] = _dot(x_ref[...], w_ref[...]).astype(o_ref.dtype)


def matmul(x, w, out_dtype, tm, tn):
    m, k = x.shape
    n = w.shape[1]
    return pl.pallas_call(
        _mm_kernel,
        grid=(m // tm, n // tn),
        in_specs=[pl.BlockSpec((tm, k), lambda i, j: (i, 0)),
                  pl.BlockSpec((k, tn), lambda i, j: (0, j))],
        out_specs=pl.BlockSpec((tm, tn), lambda i, j: (i, j)),
        out_shape=jax.ShapeDtypeStruct((m, n), out_dtype),
        compiler_params=_cparams(("parallel", "arbitrary")),
        name="matmul",
    )(x, w)


def _mm_res_kernel(x_ref, w_ref, r_ref, o_ref):
    k = pl.program_id(2)

    @pl.when(k == 0)
    def _():
        o_ref[...] = r_ref[...]

    o_ref[...] += _dot(x_ref[...], w_ref[...])


def matmul_residual(x, w, res, tm, tn, tk):
    m, k = x.shape
    n = w.shape[1]
    return pl.pallas_call(
        _mm_res_kernel,
        grid=(m // tm, n // tn, k // tk),
        in_specs=[pl.BlockSpec((tm, tk), lambda i, j, kk: (i, kk)),
                  pl.BlockSpec((tk, tn), lambda i, j, kk: (kk, j)),
                  pl.BlockSpec((tm, tn), lambda i, j, kk: (i, j))],
        out_specs=pl.BlockSpec((tm, tn), lambda i, j, kk: (i, j)),
        out_shape=jax.ShapeDtypeStruct((m, n), F32),
        compiler_params=_cparams(("parallel", "arbitrary", "arbitrary")),
        name="matmul_residual",
    )(x, w, res)


def _ple_kernel(x_ref, wg_ref, p_ref, wp_ref, r_ref, o_ref):
    gate = jax.nn.sigmoid(_dot(x_ref[...], wg_ref[...]))
    o_ref[...] = r_ref[...] + gate * _dot(p_ref[...], wp_ref[...])


def ple_update(xn, wg, pe, wp, res, tm, tn):
    m, k = xn.shape
    n = wg.shape[1]
    kp = pe.shape[1]
    return pl.pallas_call(
        _ple_kernel,
        grid=(m // tm, n // tn),
        in_specs=[pl.BlockSpec((tm, k), lambda i, j: (i, 0)),
                  pl.BlockSpec((k, tn), lambda i, j: (0, j)),
                  pl.BlockSpec((tm, kp), lambda i, j: (i, 0)),
                  pl.BlockSpec((kp, tn), lambda i, j: (0, j)),
                  pl.BlockSpec((tm, tn), lambda i, j: (i, j))],
        out_specs=pl.BlockSpec((tm, tn), lambda i, j: (i, j)),
        out_shape=jax.ShapeDtypeStruct((m, n), F32),
        compiler_params=_cparams(("parallel", "arbitrary")),
        name="ple_update",
    )(xn, wg, pe, wp, res)


def _ffn_act_kernel(g_ref, v_ref, wg_ref, wv_ref, cg0_ref, cv0_ref,
                    h_ref, cg_ref, cv_ref, sg, sv, *, tr, n_row_tiles):
    i = pl.program_id(2)
    last = FFN_CONV - 1

    def conv(x_ref, w_ref, c0_ref, c_ref, scr):
        @pl.when(i == 0)
        def _():
            scr[8 - last:8, :] = c0_ref[0]

        scr[8:8 + tr, :] = x_ref[...]
        y = scr[8:8 + tr, :] * w_ref[last:last + 1, :]
        for j in range(last):
            y = y + scr[8 - last + j:8 - last + j + tr, :] * w_ref[j:j + 1, :]
        tail = scr[8 + tr - last:8 + tr, :]
        scr[8 - last:8, :] = tail

        @pl.when(i == n_row_tiles - 1)
        def _():
            c_ref[0] = tail

        return y

    gate = conv(g_ref, wg_ref, cg0_ref, cg_ref, sg)
    val = conv(v_ref, wv_ref, cv0_ref, cv_ref, sv)
    h_ref[...] = (_silu(gate) * val).astype(h_ref.dtype)


def ffn_conv_act(up, w_conv, cache, b, t, tr, tc):
    m, f2 = up.shape
    f = f2 // 2
    nj = f // tc
    n_row_tiles = t // tr
    last = FFN_CONV - 1
    kern = functools.partial(_ffn_act_kernel, tr=tr, n_row_tiles=n_row_tiles)
    return pl.pallas_call(
        kern,
        grid=(b, nj, n_row_tiles),
        in_specs=[pl.BlockSpec((tr, tc), lambda bb, j, i: (bb * n_row_tiles + i, j)),
                  pl.BlockSpec((tr, tc), lambda bb, j, i: (bb * n_row_tiles + i, nj + j)),
                  pl.BlockSpec((FFN_CONV, tc), lambda bb, j, i: (0, j)),
                  pl.BlockSpec((FFN_CONV, tc), lambda bb, j, i: (0, nj + j)),
                  pl.BlockSpec((1, last, tc), lambda bb, j, i: (bb, 0, j)),
                  pl.BlockSpec((1, last, tc), lambda bb, j, i: (bb, 0, nj + j))],
        out_specs=[pl.BlockSpec((tr, tc), lambda bb, j, i: (bb * n_row_tiles + i, j)),
                   pl.BlockSpec((1, last, tc), lambda bb, j, i: (bb, 0, j)),
                   pl.BlockSpec((1, last, tc), lambda bb, j, i: (bb, 0, j))],
        out_shape=[jax.ShapeDtypeStruct((m, f), BF16),
                   jax.ShapeDtypeStruct((b, last, f), F32),
                   jax.ShapeDtypeStruct((b, last, f), F32)],
        scratch_shapes=[pltpu.VMEM((8 + tr, tc), F32), pltpu.VMEM((8 + tr, tc), F32)],
        compiler_params=_cparams(("parallel", "parallel", "arbitrary")),
        name="ffn_conv_act",
    )(up, up, w_conv, w_conv, cache, cache)


def _ffn_up_kernel(x_ref, wg_ref, wv_ref, cwg_ref, cwv_ref, cg0_ref, cv0_ref,
                   h_ref, cg_ref, cv_ref, carry_g, carry_v, sg, sv, *, tm, tiles_per_seq):
    i = pl.program_id(0)
    j = pl.program_id(1)
    last = FFN_CONV - 1
    first = (i % tiles_per_seq) == 0
    x = x_ref[...]

    @pl.when(first)
    def _():
        sg[8 - last:8, :] = cg0_ref[0]
        sv[8 - last:8, :] = cv0_ref[0]

    @pl.when(jnp.logical_not(first))
    def _():
        sg[8 - last:8, :] = carry_g[j, 8 - last:8, :]
        sv[8 - last:8, :] = carry_v[j, 8 - last:8, :]

    def conv(w_ref, cw_ref, c_ref, carry, scr):
        up = _dot(x, w_ref[...])
        scr[8:8 + tm, :] = up
        y = up * cw_ref[last:last + 1, :]
        for jj in range(last):
            y = y + scr[8 - last + jj:8 - last + jj + tm, :] * cw_ref[jj:jj + 1, :]
        tail = up[tm - last:tm, :]
        carry[j, 8 - last:8, :] = tail
        c_ref[0] = tail
        return y

    gate = conv(wg_ref, cwg_ref, cg_ref, carry_g, sg)
    val = conv(wv_ref, cwv_ref, cv_ref, carry_v, sv)
    h_ref[...] = (_silu(gate) * val).astype(h_ref.dtype)


def ffn_up_fused(xn, w_up, w_conv, cache, b, t, tm, tn):
    m, d = xn.shape
    fp = w_up.shape[1] // 2
    nj = fp // tn
    tiles_per_seq = t // tm
    last = FFN_CONV - 1
    kern = functools.partial(_ffn_up_kernel, tm=tm, tiles_per_seq=tiles_per_seq)
    cspec = lambda off: pl.BlockSpec((1, last, tn), lambda i, j: (i // tiles_per_seq, 0, off + j))
    tail_spec = pl.BlockSpec((1, last, tn), lambda i, j: (i, 0, j))
    hid, tg, tv = pl.pallas_call(
        kern,
        grid=(m // tm, nj),
        in_specs=[pl.BlockSpec((tm, d), lambda i, j: (i, 0)),
                  pl.BlockSpec((d, tn), lambda i, j: (0, j)),
                  pl.BlockSpec((d, tn), lambda i, j: (0, nj + j)),
                  pl.BlockSpec((FFN_CONV, tn), lambda i, j: (0, j)),
                  pl.BlockSpec((FFN_CONV, tn), lambda i, j: (0, nj + j)),
                  cspec(0), cspec(nj)],
        out_specs=[pl.BlockSpec((tm, tn), lambda i, j: (i, j)), tail_spec, tail_spec],
        out_shape=[jax.ShapeDtypeStruct((m, fp), BF16),
                   jax.ShapeDtypeStruct((m // tm, last, fp), F32),
                   jax.ShapeDtypeStruct((m // tm, last, fp), F32)],
        scratch_shapes=[pltpu.VMEM((nj, 8, tn), F32), pltpu.VMEM((nj, 8, tn), F32),
                        pltpu.VMEM((8 + tm, tn), F32), pltpu.VMEM((8 + tm, tn), F32)],
        compiler_params=_cparams(("arbitrary", "arbitrary")),
        name="ffn_up_fused",
    )(xn, w_up, w_up, w_conv, w_conv, cache, cache)
    return hid, tg[tiles_per_seq - 1::tiles_per_seq], tv[tiles_per_seq - 1::tiles_per_seq]


def _levels(c):
    out, h = [], c // 2
    while h >= 1:
        out.append(h)
        h //= 2
    return out


def _prefix_matrix(c):
    blocks = [np.tril(np.ones((c, c), np.float32))]
    for h in _levels(c):
        p = np.zeros((c, c), np.float32)
        for t in range(c):
            mid = (t // (2 * h)) * 2 * h + h
            if t % (2 * h) >= h:
                p[t, mid:t + 1] = 1.0
            else:
                p[t, t + 1:mid] = 1.0
        blocks.append(p)
    return np.concatenate(blocks, axis=0)


def _gated_head_out(o, gate, w):
    o = o * lax.rsqrt(jnp.mean(o * o, axis=-1, keepdims=True) + EPS) * w
    return o * _silu(gate)


def _stack_heads(x, nh, w):
    return jnp.concatenate([x[:, i * w:(i + 1) * w] for i in range(nh)], axis=0)


def _diag_blocks(y, nh, c, w):
    return jnp.concatenate([y[i * c:(i + 1) * c, i * w:(i + 1) * w] for i in range(nh)], axis=0)


def _block_cols(v, nh, c):
    head = lax.broadcasted_iota(jnp.int32, (nh * c, 1), 0) >> (c.bit_length() - 1)
    return jnp.concatenate([jnp.where(head == i, v, 0.0) for i in range(nh)], axis=-1)


def _row_to_col(row):
    n = row.shape[1]
    eye = (lax.broadcasted_iota(jnp.int32, (n, n), 0) == lax.broadcasted_iota(jnp.int32, (n, n), 1))
    return jnp.sum(jnp.where(eye, row, 0.0), axis=1, keepdims=True)


def _linear_attn_block(q, k, v, g, s_cat, p_ref, c, nh, dv):
    dk = q.shape[1] // nh
    r = nh * c
    z = _sel_dot(p_ref[...], g)
    b = z[0:c]
    row = lax.broadcasted_iota(jnp.int32, (c, 1), 0)
    ri = lax.broadcasted_iota(jnp.int32, (r, r), 0)
    ci = lax.broadcasted_iota(jnp.int32, (r, r), 1)
    qk = _stack_heads(q * k, nh, dk)
    attn = jnp.where(ri == ci, jnp.sum(qk, axis=-1, keepdims=True), 0.0)
    for li, h in enumerate(_levels(c)):
        e = jnp.exp(z[(li + 1) * c:(li + 2) * c])
        lower = (row & (2 * h - 1)) >= h
        ql = _stack_heads(jnp.where(lower, q * e, 0.0), nh, dk).astype(BF16)
        kl = _stack_heads(jnp.where(lower, 0.0, k * e), nh, dk).astype(BF16)
        sh = (2 * h).bit_length() - 1
        attn = attn + jnp.where((ri >> sh) == (ci >> sh), _dot_nt(ql, kl), 0.0)
    vs = _stack_heads(v, nh, dv)
    qe = _stack_heads(q * jnp.exp(b), nh, dk)
    o = _diag_blocks(_mm(qe, s_cat), nh, c, dv) + _mm(attn, vs)
    b_last = b[c - 1:c]
    kt = _stack_heads(k * jnp.exp(b_last - b), nh, dk).astype(BF16)
    upd = _dot_tn(kt, _block_cols(vs, nh, c).astype(BF16))
    el = jnp.exp(b_last)
    dec = jnp.concatenate(
        [jnp.broadcast_to(_row_to_col(el[:, i * dk:(i + 1) * dk]), (dk, dv)) for i in range(nh)], axis=-1)
    return o, dec * s_cat + upd


def _gla_kernel(q_ref, k_ref, v_ref, gg_ref, sm_ref, wg_ref, bg_ref, nw_ref, p_ref, s0_ref,
                o_ref, s_ref, *, c):
    ci = pl.program_id(1)

    @pl.when(ci == 0)
    def _():
        s_ref[...] = s0_ref[...]

    x = _dot_sel(sm_ref[...], wg_ref[...]) + bg_ref[...]
    g = _log_sigmoid(x) * (1.0 / GLA_GATE_NORM)
    q = q_ref[...] * (GLA_DK ** -0.5)
    gate = gg_ref[...]
    s_cat = jnp.concatenate([s_ref[0, hh] for hh in range(GLA_H)], axis=-1)
    o, s_new = _linear_attn_block(q, k_ref[...], v_ref[...], g, s_cat, p_ref, c, GLA_H, GLA_DV)
    for hh in range(GLA_H):
        vs = slice(hh * GLA_DV, (hh + 1) * GLA_DV)
        s_ref[0, hh] = s_new[:, vs]
        o_ref[:, vs] = _gated_head_out(o[hh * c:(hh + 1) * c], gate[:, vs],
                                       nw_ref[...]).astype(o_ref.dtype)


def gla_mixer(proj, small, wg_pad, bg, norm_w, state, b, t, c):
    m = proj.shape[0]
    nc = t // c
    kw = GLA_H * GLA_DK
    vw = GLA_H * GLA_DV
    p = jnp.asarray(_prefix_matrix(c), BF16)
    row = lambda bb, cc: bb * nc + cc
    kern = functools.partial(_gla_kernel, c=c)
    return pl.pallas_call(
        kern,
        grid=(b, nc),
        in_specs=[pl.BlockSpec((c, kw), lambda bb, cc: (row(bb, cc), 0)),
                  pl.BlockSpec((c, kw), lambda bb, cc: (row(bb, cc), 1)),
                  pl.BlockSpec((c, vw), lambda bb, cc: (row(bb, cc), 1)),
                  pl.BlockSpec((c, vw), lambda bb, cc: (row(bb, cc), 2)),
                  pl.BlockSpec((c, LANE), lambda bb, cc: (row(bb, cc), 0)),
                  pl.BlockSpec((LANE, kw), lambda bb, cc: (0, 0)),
                  pl.BlockSpec((1, kw), lambda bb, cc: (0, 0)),
                  pl.BlockSpec((1, GLA_DV), lambda bb, cc: (0, 0)),
                  pl.BlockSpec(p.shape, lambda bb, cc: (0, 0)),
                  pl.BlockSpec((1, GLA_H, GLA_DK, GLA_DV), lambda bb, cc: (bb, 0, 0, 0))],
        out_specs=[pl.BlockSpec((c, vw), lambda bb, cc: (row(bb, cc), 0)),
                   pl.BlockSpec((1, GLA_H, GLA_DK, GLA_DV), lambda bb, cc: (bb, 0, 0, 0))],
        out_shape=[jax.ShapeDtypeStruct((m, vw), BF16),
                   jax.ShapeDtypeStruct((b, GLA_H, GLA_DK, GLA_DV), F32)],
        compiler_params=_cparams(("parallel", "arbitrary")),
        name="gla_mixer",
    )(proj, proj, proj, proj, small, wg_pad, bg.reshape(1, -1), norm_w.reshape(1, -1), p, state)


HG_HB = 4


def _hgrn_kernel(q_ref, f_ref, i_ref, g_ref, lb_ref, nw_ref, p_ref, s0_ref,
                 o_ref, s_ref, *, c):
    ci = pl.program_id(2)

    @pl.when(ci == 0)
    def _():
        s_ref[...] = s0_ref[...]

    lb = lb_ref[...]
    log_lb = jnp.log(jnp.maximum(lb, TINY))
    zf = f_ref[...]
    a = log_lb
    cc = jnp.log1p(-lb) + _log_sigmoid(zf)
    log_f = jnp.maximum(a, cc) + jnp.log1p(jnp.exp(-jnp.abs(a - cc)))
    key = (1.0 - lb) * jax.nn.sigmoid(-zf)
    q = _silu(q_ref[...])
    v = i_ref[...]
    gate = g_ref[...]
    s_cat = jnp.concatenate([s_ref[0, hh] for hh in range(HG_HB)], axis=-1)
    o, s_new = _linear_attn_block(q, key, v, log_f, s_cat, p_ref, c, HG_HB, HG_DV)
    for hh in range(HG_HB):
        sl = slice(hh * HG_DV, (hh + 1) * HG_DV)
        s_ref[0, hh] = s_new[:, sl]
        o_ref[:, sl] = _gated_head_out(o[hh * c:(hh + 1) * c], gate[:, sl],
                                       nw_ref[...]).astype(o_ref.dtype)


def hgrn_mixer(proj, lb, norm_w, state, b, t, c, col0):
    m = proj.shape[0]
    nc = t // c
    ng = HG_H // HG_HB
    w = HG_HB * HG_DK
    p = jnp.asarray(_prefix_matrix(c), BF16)
    row = lambda bb, g, cc: bb * nc + cc
    kern = functools.partial(_hgrn_kernel, c=c)
    return pl.pallas_call(
        kern,
        grid=(b, ng, nc),
        in_specs=[pl.BlockSpec((c, w), lambda bb, g, cc: (row(bb, g, cc), col0 + g)),
                  pl.BlockSpec((c, w), lambda bb, g, cc: (row(bb, g, cc), col0 + ng + g)),
                  pl.BlockSpec((c, w), lambda bb, g, cc: (row(bb, g, cc), col0 + 2 * ng + g)),
                  pl.BlockSpec((c, w), lambda bb, g, cc: (row(bb, g, cc), col0 + 3 * ng + g)),
                  pl.BlockSpec((1, w), lambda bb, g, cc: (0, g)),
                  pl.BlockSpec((1, HG_DV), lambda bb, g, cc: (0, 0)),
                  pl.BlockSpec(p.shape, lambda bb, g, cc: (0, 0)),
                  pl.BlockSpec((1, HG_HB, HG_DK, HG_DV), lambda bb, g, cc: (bb, g, 0, 0))],
        out_specs=[pl.BlockSpec((c, w), lambda bb, g, cc: (row(bb, g, cc), g)),
                   pl.BlockSpec((1, HG_HB, HG_DK, HG_DV), lambda bb, g, cc: (bb, g, 0, 0))],
        out_shape=[jax.ShapeDtypeStruct((m, HG_H * HG_DV), BF16),
                   jax.ShapeDtypeStruct((b, HG_H, HG_DK, HG_DV), F32)],
        compiler_params=_cparams(("parallel", "parallel", "arbitrary")),
        name="hgrn_mixer",
    )(proj, proj, proj, proj, lb.reshape(1, -1), norm_w.reshape(1, -1), p, state)


GDN_HB = 4
GDN_GS = 2


def _each(f, *lists):
    return [f(*args) for args in zip(*lists)]


def _unit_lower_inverse(ns, r, c):
    ri = lax.broadcasted_iota(jnp.int32, (r, r), 0)
    ci = lax.broadcasted_iota(jnp.int32, (r, r), 1)
    eye = (ri == ci).astype(F32)
    d = [jnp.where((ri >> 4) == (ci >> 4), n, 0.0) for n in ns]
    x = [eye - d_ for d_ in d]
    p = d
    for _ in range(3):
        p = _each(_mm, p, p)
        x = _each(lambda x_, p_: x_ + _mm(x_, p_), x, p)
    size = 16
    while size < c:
        sh = size.bit_length() - 1
        mask = ((ri >> (sh + 1)) == (ci >> (sh + 1))) & ((ri >> sh) > (ci >> sh))
        lx = _each(lambda n, x_: _mm(jnp.where(mask, n, 0.0), x_), ns, x)
        x = _each(lambda x_, lx_: x_ - _mm(x_, lx_), x, lx)
        size *= 2
    return x


def _gdn_blocks(q, k, v, beta, b_c, b_last, s_cat, c, nh):
    r = nh * c
    ri = lax.broadcasted_iota(jnp.int32, (r, r), 0)
    ci = lax.broadcasted_iota(jnp.int32, (r, r), 1)
    shc = c.bit_length() - 1
    causal = ((ri >> shc) == (ci >> shc)) & (ri >= ci)

    def decay_of(bc):
        b_r = jnp.sum(jnp.where(ri == ci, bc, 0.0), axis=0, keepdims=True)
        return jnp.where(causal, jnp.exp(jnp.minimum(bc - b_r, 0.0)), 0.0)

    decay = _each(decay_of, b_c)
    kb = [k_.astype(BF16) for k_ in k]
    kq = _each(lambda kb_, q_: _dot_nt(jnp.concatenate([kb_, q_.astype(BF16)], axis=0), kb_), kb, q)
    n = _each(lambda be, kq_, de: jnp.where(ri > ci, be * kq_[:r] * de, 0.0), beta, kq, decay)
    tinv = _unit_lower_inverse(n, r, c)
    eb = [jnp.exp(bc) for bc in b_c]
    sol = _each(lambda t_, v_, k_, be, e_: _mm(t_, jnp.concatenate([v_ * be, k_ * (be * e_)], axis=-1)),
                tinv, v, k, beta, eb)
    ws_qs = _each(lambda so, q_, e_, s_: _mm(jnp.concatenate([so[:, GDN_DV:], q_ * e_], axis=0), s_),
                  sol, q, eb, s_cat)
    v_new = _each(lambda so, wq: so[:, :GDN_DV] - _diag_blocks(wq[:r], nh, c, GDN_DV), sol, ws_qs)
    o = _each(lambda wq, kq_, de, vn: _diag_blocks(wq[r:], nh, c, GDN_DV) + _mm(kq_[r:] * de, vn),
              ws_qs, kq, decay, v_new)
    upd = _each(lambda k_, bl, bc, vn: _dot_tn((k_ * jnp.exp(bl - bc)).astype(BF16),
                                               _block_cols(vn, nh, c).astype(BF16)),
                k, b_last, b_c, v_new)

    def decayed(bl, s_, up):
        el = jnp.exp(bl)
        dec = jnp.concatenate(
            [jnp.broadcast_to(el[i * c:i * c + 1], (1, GDN_DV)) for i in range(nh)], axis=-1)
        return dec * s_ + up

    return o, _each(decayed, b_last, s_cat, upd)


def _gdn_kernel(q_ref, k_ref, v_ref, z_ref, gc_ref, wq_ref, wk_ref, wv_ref,
                pc_ref, nw_ref, tril_ref, s0_ref, cq0_ref, ck0_ref, cv0_ref,
                o_ref, s_ref, cq_ref, ck_ref, cv_ref,
                xq, xk, xv, *, c, nc):
    ci = pl.program_id(2)
    last = GDN_CONV - 1

    @pl.when(ci == 0)
    def _():
        s_ref[...] = s0_ref[...]

    def conv(x_ref, w_ref, c0_ref, c_ref, scr):
        @pl.when(ci == 0)
        def _():
            scr[8 - last:8, :] = c0_ref[0]

        scr[8:8 + c, :] = x_ref[...]
        y = scr[8:8 + c, :] * w_ref[last:last + 1, :]
        for j in range(last):
            y = y + scr[8 - last + j:8 - last + j + c, :] * w_ref[j:j + 1, :]
        tail = scr[8 + c - last:8 + c, :]
        scr[8 - last:8, :] = tail

        @pl.when(ci == nc - 1)
        def _():
            c_ref[0] = tail

        return _silu(y)

    qa = conv(q_ref, wq_ref, cq0_ref, cq_ref, xq)
    ka = conv(k_ref, wk_ref, ck0_ref, ck_ref, xk)
    va = conv(v_ref, wv_ref, cv0_ref, cv_ref, xv)

    nh = GDN_HB
    gw = nh * GDN_DK
    q, k, v, beta, b_c, b_last, s_cat = [], [], [], [], [], [], []
    for gi in range(GDN_GS):
        gcol = gc_ref[gi]
        pc = pc_ref[gi]
        beta_all = jax.nn.sigmoid(gcol)
        g_c = pc[0:1, :] * _softplus(gcol + pc[1:2, :])
        b_col = _sel_dot(tril_ref[...], g_c)
        beta.append(jnp.concatenate([beta_all[:, hh:hh + 1] for hh in range(nh)], axis=0))
        b_c.append(jnp.concatenate([b_col[:, nh + hh:nh + hh + 1] for hh in range(nh)], axis=0))
        b_last.append(jnp.concatenate(
            [jnp.broadcast_to(b_col[c - 1:c, nh + hh:nh + hh + 1], (c, 1)) for hh in range(nh)],
            axis=0))
        gs = slice(gi * gw, (gi + 1) * gw)
        qg = _stack_heads(qa[:, gs], nh, GDN_DK)
        kg = _stack_heads(ka[:, gs], nh, GDN_DK)
        q.append(qg * lax.rsqrt(jnp.sum(qg * qg, axis=-1, keepdims=True) + EPS) * (GDN_DK ** -0.5))
        k.append(kg * lax.rsqrt(jnp.sum(kg * kg, axis=-1, keepdims=True) + EPS))
        v.append(_stack_heads(va[:, gs], nh, GDN_DV))
        s_cat.append(jnp.concatenate([s_ref[0, gi * nh + hh] for hh in range(nh)], axis=-1))
    o, s_new = _gdn_blocks(q, k, v, beta, b_c, b_last, s_cat, c, nh)
    z = z_ref[...]
    for gi in range(GDN_GS):
        for hh in range(nh):
            sl = slice(hh * GDN_DV, (hh + 1) * GDN_DV)
            zl = slice(gi * gw + hh * GDN_DV, gi * gw + (hh + 1) * GDN_DV)
            s_ref[0, gi * nh + hh] = s_new[gi][:, sl]
            o_ref[:, zl] = _gated_head_out(o[gi][hh * c:(hh + 1) * c], z[:, zl],
                                           nw_ref[...]).astype(o_ref.dtype)


def gdn_mixer(proj, gates_c, conv_w, par_c, norm_w, state, cache, b, t, c, col0):
    m = proj.shape[0]
    nc = t // c
    hs = GDN_HB * GDN_GS
    ng = GDN_H // hs
    w = hs * GDN_DK
    col0 = col0 // GDN_GS
    last = GDN_CONV - 1
    tril = jnp.asarray(np.tril(np.ones((c, c), np.float32)), BF16)
    row = lambda bb, g, cc: bb * nc + cc
    kern = functools.partial(_gdn_kernel, c=c, nc=nc)
    big = lambda off: pl.BlockSpec((c, w), lambda bb, g, cc: (row(bb, g, cc), col0 + off * ng + g))
    cw = lambda off: pl.BlockSpec((GDN_CONV, w), lambda bb, g, cc: (0, off * ng + g))
    c0 = lambda off: pl.BlockSpec((1, last, w), lambda bb, g, cc: (bb, 0, off * ng + g))
    cout = pl.BlockSpec((1, last, w), lambda bb, g, cc: (bb, 0, g))
    outs = pl.pallas_call(
        kern,
        grid=(b, ng, nc),
        in_specs=[big(0), big(1), big(2), big(3),
                  pl.BlockSpec((GDN_GS, c, LANE), lambda bb, g, cc: (g, row(bb, g, cc), 0)),
                  cw(0), cw(1), cw(2),
                  pl.BlockSpec((GDN_GS, 2, LANE), lambda bb, g, cc: (g, 0, 0)),
                  pl.BlockSpec((1, GDN_DV), lambda bb, g, cc: (0, 0)),
                  pl.BlockSpec((c, c), lambda bb, g, cc: (0, 0)),
                  pl.BlockSpec((1, hs, GDN_DK, GDN_DV), lambda bb, g, cc: (bb, g, 0, 0)),
                  c0(0), c0(1), c0(2)],
        out_specs=[pl.BlockSpec((c, w), lambda bb, g, cc: (row(bb, g, cc), g)),
                   pl.BlockSpec((1, hs, GDN_DK, GDN_DV), lambda bb, g, cc: (bb, g, 0, 0)),
                   cout, cout, cout],
        out_shape=[jax.ShapeDtypeStruct((m, GDN_H * GDN_DV), BF16),
                   jax.ShapeDtypeStruct((b, GDN_H, GDN_DK, GDN_DV), F32),
                   jax.ShapeDtypeStruct((b, last, GDN_H * GDN_DK), F32),
                   jax.ShapeDtypeStruct((b, last, GDN_H * GDN_DK), F32),
                   jax.ShapeDtypeStruct((b, last, GDN_H * GDN_DV), F32)],
        scratch_shapes=[pltpu.VMEM((8 + c, w), F32), pltpu.VMEM((8 + c, w), F32),
                        pltpu.VMEM((8 + c, w), F32)],
        compiler_params=_cparams(("parallel", "parallel", "arbitrary")),
        name="gdn_mixer",
    )(proj, proj, proj, proj, gates_c, conv_w, conv_w, conv_w, par_c,
      norm_w.reshape(1, -1), tril, state, cache, cache, cache)
    o, s, cq, ck, cv = outs
    return o, s, jnp.concatenate([cq, ck, cv], axis=-1)


def _row_tile(m, cap):
    tm = min(m, cap)
    while m % tm:
        tm //= 2
    return tm


def _group_gates(small):
    m = small.shape[0]
    ng = GDN_H // GDN_HB
    db = small[:, GLA_RANK:GLA_RANK + GDN_H].reshape(m, ng, GDN_HB)
    da = small[:, GLA_RANK + GDN_H:GLA_RANK + 2 * GDN_H].reshape(m, ng, GDN_HB)
    g = jnp.concatenate([db, da, jnp.zeros((m, ng, LANE - 2 * GDN_HB), F32)], axis=-1)
    return jnp.transpose(g, (1, 0, 2))


def _group_params(a_log, dt_bias):
    ng = GDN_H // GDN_HB
    na = -jnp.exp(a_log.astype(F32)).reshape(ng, GDN_HB)
    dt = dt_bias.astype(F32).reshape(ng, GDN_HB)
    pad = jnp.zeros((ng, GDN_HB), F32)
    tail = jnp.zeros((ng, LANE - 2 * GDN_HB), F32)
    return jnp.stack([jnp.concatenate([pad, na, tail], axis=-1),
                      jnp.concatenate([pad, dt, tail], axis=-1)], axis=1)


def _prep_weights(w_in, w_gla_gate, w_out, w_up, w_down, w_ple_gate, w_ple_proj):
    d = w_in.shape[1]
    c0 = 2 * GLA_H * GLA_DK + 2 * GLA_H * GLA_DV
    c1 = c0 + GLA_RANK
    nb = 2 * GDN_H * GDN_DK + 2 * GDN_H * GDN_DV
    c2 = c1 + nb
    c3 = c2 + 2 * GDN_H
    cast = lambda a: a.astype(BF16)
    w_in = cast(w_in)
    w_main = jnp.concatenate([w_in[:, :, :c0], w_in[:, :, c1:c2], w_in[:, :, c3:]], axis=-1)
    w_small = jnp.concatenate([w_in[:, :, c0:c1], w_in[:, :, c2:c3]], axis=-1)
    w_small = jnp.pad(w_small, ((0, 0), (0, 0), (0, LANE - w_small.shape[-1])))
    wg_pad = jnp.pad(cast(w_gla_gate), ((0, 0), (0, LANE - GLA_RANK), (0, 0)))
    w_up = _pad_halves(cast(w_up))
    w_down = jnp.pad(cast(w_down), ((0, 0), (0, w_up.shape[-1] // 2 - w_down.shape[1]), (0, 0)))
    return (w_main, w_small, wg_pad, cast(w_out), w_up, w_down, cast(w_ple_gate), cast(w_ple_proj))


FFN_TILE = 1024


def _pad_halves(a):
    f = a.shape[-1] // 2
    fp = -(-f // FFN_TILE) * FFN_TILE
    pad = [(0, 0)] * (a.ndim - 1) + [(0, fp - f)]
    return jnp.concatenate([jnp.pad(a[..., :f], pad), jnp.pad(a[..., f:], pad)], axis=-1)


def _trunk(x, pe, st_gla, st_gdn, cb_gdn, st_hg, cb_ffn, wts, prm, lower_bounds, norm_final):
    (w_main, w_small, wg_pad, w_out, w_up, w_down, w_pg, w_pp) = wts
    (norm_mix, b_gla_gate, gla_norm, w_gdn_conv, gdn_a_log, gdn_dt_bias, gdn_norm, hgrn_norm,
     norm_ffn, w_ffn_conv, norm_ple) = prm
    b, t, d = x.shape
    depth = w_main.shape[0]
    m = b * t
    c = min(MAX_CHUNK, t)
    tm = _row_tile(m, ROW_TILE)
    tr = _row_tile(t, 256)
    h = x.reshape(m, d)
    n_gla, n_gdn, n_gconv, n_hg, n_fconv = [], [], [], [], []
    for li in range(depth):
        xn = rmsnorm(h, norm_mix[li], BF16, _row_tile(m, 256))
        proj = matmul(xn, w_main[li], F32, tm, 512)
        small = matmul(xn, w_small[li], F32, tm, LANE)

        o_gla, s_gla = gla_mixer(proj, small, wg_pad[li], b_gla_gate[li], gla_norm[li],
                                 st_gla[li], b, t, c)
        gdn_col0 = (2 * GLA_H * GLA_DK + 2 * GLA_H * GLA_DV) // (GDN_HB * GDN_DK)
        o_gdn, s_gdn, b_gdn = gdn_mixer(proj, _group_gates(small), w_gdn_conv[li],
                                        _group_params(gdn_a_log[li], gdn_dt_bias[li]), gdn_norm[li],
                                        st_gdn[li], cb_gdn[li], b, t, c, gdn_col0)
        hg_col0 = gdn_col0 + (2 * GDN_H * GDN_DK + 2 * GDN_H * GDN_DV) // (HG_HB * HG_DK)
        o_hg, s_hg = hgrn_mixer(proj, lower_bounds[li], hgrn_norm[li], st_hg[li], b, t, c, hg_col0)
        mix = jnp.concatenate([o_gla, o_gdn, o_hg], axis=-1)
        h = matmul_residual(mix, w_out[li], h, tm, 512, mix.shape[1])

        xn = rmsnorm(h, norm_ffn[li], BF16, _row_tile(m, 256))
        f = cb_ffn.shape[-1] // 2
        fp = w_down.shape[1]
        conv_w = _pad_halves(w_ffn_conv[li])
        cache = _pad_halves(cb_ffn[li])
        if t % tm == 0:
            hid, cg, cv = ffn_up_fused(xn, w_up[li], conv_w, cache, b, t, tm, 512)
        else:
            up = matmul(xn, w_up[li], F32, tm, 512)
            hid, cg, cv = ffn_conv_act(up, conv_w, cache, b, t, tr, fp // 2)
        b_ffn = jnp.concatenate([cg[..., :f], cv[..., :f]], axis=-1)
        h = matmul_residual(hid, w_down[li], h, tm, 512, fp // 2)

        xn = rmsnorm(h, norm_ple[li], BF16, _row_tile(m, 256))
        h = ple_update(xn, w_pg[li], pe[li].reshape(m, -1).astype(BF16), w_pp[li], h, tm, 512)

        n_gla.append(s_gla)
        n_gdn.append(s_gdn)
        n_gconv.append(b_gdn)
        n_hg.append(s_hg)
        n_fconv.append(b_ffn)
    y = rmsnorm(h, norm_final, F32, _row_tile(m, 256)).reshape(b, t, d)
    return (y, jnp.stack(n_gla), jnp.stack(n_gdn), jnp.stack(n_gconv),
            jnp.stack(n_hg), jnp.stack(n_fconv))


def kernel(x_prompt, x_sample, p_prompt, p_sample, state_gla, state_gdn, cache_gdn_conv, state_hgrn, cache_ffn_conv, norm_mix, w_in, w_gla_gate, b_gla_gate, gla_norm, w_gdn_conv, gdn_a_log, gdn_dt_bias, gdn_norm, hgrn_lb, hgrn_norm, w_out, norm_ffn, w_up, w_ffn_conv, w_down, norm_ple, w_ple_gate, w_ple_proj, norm_final):
    depth = w_in.shape[0]
    bp = x_prompt.shape[0]
    wts = _prep_weights(w_in, w_gla_gate, w_out, w_up, w_down, w_ple_gate, w_ple_proj)
    prm = (norm_mix, b_gla_gate, gla_norm, w_gdn_conv, gdn_a_log, gdn_dt_bias, gdn_norm,
           hgrn_norm, norm_ffn, w_ffn_conv, norm_ple)
    sm = jax.nn.softmax(hgrn_lb.astype(F32), axis=0)
    lower_bounds = jnp.cumsum(sm, axis=0) - sm[0]
    zeros = lambda a: jnp.zeros((depth, bp) + a.shape[2:], F32)
    prompt = _trunk(x_prompt, p_prompt, zeros(state_gla), zeros(state_gdn), zeros(cache_gdn_conv),
                    zeros(state_hgrn), zeros(cache_ffn_conv), wts, prm, lower_bounds, norm_final)
    sample = _trunk(x_sample, p_sample, state_gla, state_gdn, cache_gdn_conv, state_hgrn,
                    cache_ffn_conv, wts, prm, lower_bounds, norm_final)
    return (prompt[0], sample[0]) + prompt[1:] + sample[1:]
```

```python
import functools

import numpy as np
import jax
import jax.numpy as jnp
from jax import lax
from jax.experimental import pallas as pl
from jax.experimental.pallas import tpu as pltpu

F32 = jnp.float32
BF16 = jnp.bfloat16

EPS = 1e-6
TINY = 1e-30
GLA_GATE_NORM = 16.0
GLA_H, GLA_DK, GLA_DV, GLA_RANK = 4, 128, 256, 16
GDN_H, GDN_DK, GDN_DV, GDN_CONV = 16, 128, 128, 4
HG_H, HG_DK, HG_DV = 8, 128, 128
FFN_CONV = 3
MAX_CHUNK = 64
ROW_TILE = 2048

LANE = 128
VMEM_LIMIT = 56 * 1024 * 1024


def _cparams(sem):
    return pltpu.CompilerParams(dimension_semantics=sem, vmem_limit_bytes=VMEM_LIMIT)


def _dot(a, b):
    return jnp.dot(a, b, preferred_element_type=F32)


def _dot_nt(a, b):
    return lax.dot_general(a, b, (((1,), (1,)), ((), ())), preferred_element_type=F32)


def _dot_tn(a, b):
    return lax.dot_general(a, b, (((0,), (0,)), ((), ())), preferred_element_type=F32)


def _mm(a, b):
    return _dot(a.astype(BF16), b.astype(BF16))


def _split3(x):
    hi = x.astype(BF16)
    r = x - hi.astype(F32)
    mid = r.astype(BF16)
    lo = (r - mid.astype(F32)).astype(BF16)
    return hi, mid, lo


def _sel_dot(p, x):
    hi, mid, lo = _split3(x)
    return _dot(p, hi) + _dot(p, mid) + _dot(p, lo)


def _dot_sel(x, p):
    hi, mid, lo = _split3(x)
    return _dot(hi, p) + _dot(mid, p) + _dot(lo, p)


def _softplus(x):
    return jnp.maximum(x, 0.0) + jnp.log1p(jnp.exp(-jnp.abs(x)))


def _log_sigmoid(x):
    return -_softplus(-x)


def _silu(x):
    return x * jax.nn.sigmoid(x)


def _rmsnorm_kernel(x_ref, w_ref, o_ref):
    x = x_ref[...]
    y = x * lax.rsqrt(jnp.mean(x * x, axis=-1, keepdims=True) + EPS)
    o_ref[...] = (y * w_ref[...]).astype(o_ref.dtype)


def rmsnorm(x, w, out_dtype, tm):
    m, d = x.shape
    return pl.pallas_call(
        _rmsnorm_kernel,
        grid=(m // tm,),
        in_specs=[pl.BlockSpec((tm, d), lambda i: (i, 0)),
                  pl.BlockSpec((1, d), lambda i: (0, 0))],
        out_specs=pl.BlockSpec((tm, d), lambda i: (i, 0)),
        out_shape=jax.ShapeDtypeStruct((m, d), out_dtype),
        compiler_params=_cparams(("parallel",)),
        name="rmsnorm",
    )(x, w.reshape(1, d))


SINGLE_BUFFER_BYTES = 16 * 1024 * 1024


def _act_spec(shape, index_map, itemsize=2):
    if shape[0] * shape[1] * itemsize >= SINGLE_BUFFER_BYTES:
        return pl.BlockSpec(shape, index_map, pipeline_mode=pl.Buffered(1))
    return pl.BlockSpec(shape, index_map)


def _mm_kernel(x_ref, w_ref, o_ref):
    o_ref[...] = _dot(x_ref[...], w_ref[...].astype(BF16)).astype(o_ref.dtype)


def _mm_shift_kernel(x_ref, wa_ref, wb_ref, o_ref, *, shift):
    tn = wa_ref.shape[1]
    w = jnp.concatenate([wa_ref[...], wb_ref[...]], axis=1)[:, shift:shift + tn]
    o_ref[...] = _dot(x_ref[...], w.astype(BF16)).astype(o_ref.dtype)


def matmul(x, w, li, out_dtype, tm, tn, col0=0, ncols=None):
    m, k = x.shape
    ncols = w.shape[2] if ncols is None else ncols
    shift = col0 % LANE
    base = (col0 - shift) // tn
    assert (col0 - shift) % tn == 0 and ncols % tn == 0
    x_spec = _act_spec((tm, k), lambda i, j: (i, 0))
    wa_spec = pl.BlockSpec((None, k, tn), lambda i, j: (li, 0, base + j))
    common = dict(
        grid=(m // tm, ncols // tn),
        out_specs=pl.BlockSpec((tm, tn), lambda i, j: (i, j)),
        out_shape=jax.ShapeDtypeStruct((m, ncols), out_dtype),
        compiler_params=_cparams(("parallel", "arbitrary")),
        name="matmul",
    )
    if shift == 0:
        return pl.pallas_call(_mm_kernel, in_specs=[x_spec, wa_spec], **common)(x, w)
    lanes_per_tile = tn // LANE
    wb_spec = pl.BlockSpec((None, k, LANE), lambda i, j: (li, 0, (base + j + 1) * lanes_per_tile))
    return pl.pallas_call(functools.partial(_mm_shift_kernel, shift=shift),
                          in_specs=[x_spec, wa_spec, wb_spec], **common)(x, w, w)


def _mm_res_kernel(x_ref, w_ref, r_ref, o_ref):
    k = pl.program_id(2)

    @pl.when(k == 0)
    def _():
        o_ref[...] = r_ref[...]

    o_ref[...] += _dot(x_ref[...], w_ref[...].astype(BF16))


def matmul_residual(x, w, li, res, tm, tn, tk):
    m, k = x.shape
    n = w.shape[2]
    return pl.pallas_call(
        _mm_res_kernel,
        grid=(m // tm, n // tn, k // tk),
        in_specs=[_act_spec((tm, tk), lambda i, j, kk: (i, kk)) if k == tk
                  else pl.BlockSpec((tm, tk), lambda i, j, kk: (i, kk)),
                  pl.BlockSpec((None, tk, tn), lambda i, j, kk: (li, kk, j)),
                  pl.BlockSpec((tm, tn), lambda i, j, kk: (i, j))],
        out_specs=pl.BlockSpec((tm, tn), lambda i, j, kk: (i, j)),
        out_shape=jax.ShapeDtypeStruct((m, n), F32),
        compiler_params=_cparams(("parallel", "arbitrary", "arbitrary")),
        name="matmul_residual",
    )(x, w, res)


def _ple_kernel(x_ref, wg_ref, p_ref, wp_ref, r_ref, o_ref):
    gate = jax.nn.sigmoid(_dot(x_ref[...], wg_ref[...].astype(BF16)))
    o_ref[...] = r_ref[...] + gate * _dot(p_ref[...], wp_ref[...].astype(BF16))


def ple_update(xn, wg, pe, wp, li, res, tm, tn):
    m, k = xn.shape
    n = wg.shape[2]
    kp = pe.shape[1]
    return pl.pallas_call(
        _ple_kernel,
        grid=(m // tm, n // tn),
        in_specs=[_act_spec((tm, k), lambda i, j: (i, 0)),
                  pl.BlockSpec((None, k, tn), lambda i, j: (li, 0, j)),
                  pl.BlockSpec((tm, kp), lambda i, j: (i, 0)),
                  pl.BlockSpec((None, kp, tn), lambda i, j: (li, 0, j)),
                  pl.BlockSpec((tm, tn), lambda i, j: (i, j))],
        out_specs=pl.BlockSpec((tm, tn), lambda i, j: (i, j)),
        out_shape=jax.ShapeDtypeStruct((m, n), F32),
        compiler_params=_cparams(("parallel", "arbitrary")),
        name="ple_update",
    )(xn, wg, pe, wp, res)


def _ffn_act_kernel(g_ref, v_ref, wg_ref, wv_ref, cg0_ref, cv0_ref,
                    h_ref, cg_ref, cv_ref, sg, sv, *, tr, n_row_tiles):
    i = pl.program_id(2)
    last = FFN_CONV - 1

    def conv(x_ref, w_ref, c0_ref, c_ref, scr):
        @pl.when(i == 0)
        def _():
            scr[8 - last:8, :] = c0_ref[0]

        scr[8:8 + tr, :] = x_ref[...]
        y = scr[8:8 + tr, :] * w_ref[last:last + 1, :]
        for j in range(last):
            y = y + scr[8 - last + j:8 - last + j + tr, :] * w_ref[j:j + 1, :]
        tail = scr[8 + tr - last:8 + tr, :]
        scr[8 - last:8, :] = tail

        @pl.when(i == n_row_tiles - 1)
        def _():
            c_ref[0] = tail

        return y

    gate = conv(g_ref, wg_ref, cg0_ref, cg_ref, sg)
    val = conv(v_ref, wv_ref, cv0_ref, cv_ref, sv)
    h_ref[...] = (_silu(gate) * val).astype(h_ref.dtype)


def ffn_conv_act(up, w_conv, cache, b, t, tr, tc):
    m, f2 = up.shape
    f = f2 // 2
    nj = f // tc
    n_row_tiles = t // tr
    last = FFN_CONV - 1
    kern = functools.partial(_ffn_act_kernel, tr=tr, n_row_tiles=n_row_tiles)
    return pl.pallas_call(
        kern,
        grid=(b, nj, n_row_tiles),
        in_specs=[pl.BlockSpec((tr, tc), lambda bb, j, i: (bb * n_row_tiles + i, j)),
                  pl.BlockSpec((tr, tc), lambda bb, j, i: (bb * n_row_tiles + i, nj + j)),
                  pl.BlockSpec((FFN_CONV, tc), lambda bb, j, i: (0, j)),
                  pl.BlockSpec((FFN_CONV, tc), lambda bb, j, i: (0, nj + j)),
                  pl.BlockSpec((1, last, tc), lambda bb, j, i: (bb, 0, j)),
                  pl.BlockSpec((1, last, tc), lambda bb, j, i: (bb, 0, nj + j))],
        out_specs=[pl.BlockSpec((tr, tc), lambda bb, j, i: (bb * n_row_tiles + i, j)),
                   pl.BlockSpec((1, last, tc), lambda bb, j, i: (bb, 0, j)),
                   pl.BlockSpec((1, last, tc), lambda bb, j, i: (bb, 0, j))],
        out_shape=[jax.ShapeDtypeStruct((m, f), BF16),
                   jax.ShapeDtypeStruct((b, last, f), F32),
                   jax.ShapeDtypeStruct((b, last, f), F32)],
        scratch_shapes=[pltpu.VMEM((8 + tr, tc), F32), pltpu.VMEM((8 + tr, tc), F32)],
        compiler_params=_cparams(("parallel", "parallel", "arbitrary")),
        name="ffn_conv_act",
    )(up, up, w_conv, w_conv, cache, cache)


def _ffn_up_kernel(x_ref, wg_ref, wv_ref, cwg_ref, cwv_ref, cg0_ref, cv0_ref,
                   h_ref, cg_ref, cv_ref, carry_g, carry_v, sg, sv, *, tm, tiles_per_seq):
    i = pl.program_id(0)
    j = pl.program_id(1)
    last = FFN_CONV - 1
    first = (i % tiles_per_seq) == 0
    x = x_ref[...]

    @pl.when(first)
    def _():
        sg[8 - last:8, :] = cg0_ref[0]
        sv[8 - last:8, :] = cv0_ref[0]

    @pl.when(jnp.logical_not(first))
    def _():
        sg[8 - last:8, :] = carry_g[j, 8 - last:8, :]
        sv[8 - last:8, :] = carry_v[j, 8 - last:8, :]

    def conv(w_ref, cw_ref, c_ref, carry, scr):
        up = _dot(x, w_ref[...].astype(BF16))
        scr[8:8 + tm, :] = up
        y = up * cw_ref[last:last + 1, :]
        for jj in range(last):
            y = y + scr[8 - last + jj:8 - last + jj + tm, :] * cw_ref[jj:jj + 1, :]
        tail = up[tm - last:tm, :]
        carry[j, 8 - last:8, :] = tail
        c_ref[0] = tail
        return y

    gate = conv(wg_ref, cwg_ref, cg_ref, carry_g, sg)
    val = conv(wv_ref, cwv_ref, cv_ref, carry_v, sv)
    h_ref[...] = (_silu(gate) * val).astype(h_ref.dtype)


def ffn_up_fused(xn, w_up, li, w_conv, cache, b, t, tm, tn):
    m, d = xn.shape
    fp = w_up.shape[2] // 2
    nj = fp // tn
    tiles_per_seq = t // tm
    last = FFN_CONV - 1
    kern = functools.partial(_ffn_up_kernel, tm=tm, tiles_per_seq=tiles_per_seq)
    cspec = lambda off: pl.BlockSpec((1, last, tn), lambda i, j: (i // tiles_per_seq, 0, off + j))
    tail_spec = pl.BlockSpec((1, last, tn), lambda i, j: (i, 0, j))
    hid, tg, tv = pl.pallas_call(
        kern,
        grid=(m // tm, nj),
        in_specs=[_act_spec((tm, d), lambda i, j: (i, 0)),
                  pl.BlockSpec((None, d, tn), lambda i, j: (li, 0, j)),
                  pl.BlockSpec((None, d, tn), lambda i, j: (li, 0, nj + j)),
                  pl.BlockSpec((FFN_CONV, tn), lambda i, j: (0, j)),
                  pl.BlockSpec((FFN_CONV, tn), lambda i, j: (0, nj + j)),
                  cspec(0), cspec(nj)],
        out_specs=[pl.BlockSpec((tm, tn), lambda i, j: (i, j)), tail_spec, tail_spec],
        out_shape=[jax.ShapeDtypeStruct((m, fp), BF16),
                   jax.ShapeDtypeStruct((m // tm, last, fp), F32),
                   jax.ShapeDtypeStruct((m // tm, last, fp), F32)],
        scratch_shapes=[pltpu.VMEM((nj, 8, tn), F32), pltpu.VMEM((nj, 8, tn), F32),
                        pltpu.VMEM((8 + tm, tn), F32), pltpu.VMEM((8 + tm, tn), F32)],
        compiler_params=_cparams(("arbitrary", "arbitrary")),
        name="ffn_up_fused",
    )(xn, w_up, w_up, w_conv, w_conv, cache, cache)
    return hid, tg[tiles_per_seq - 1::tiles_per_seq], tv[tiles_per_seq - 1::tiles_per_seq]


def _levels(c):
    out, h = [], c // 2
    while h >= 1:
        out.append(h)
        h //= 2
    return out


def _prefix_matrix(c):
    blocks = [np.tril(np.ones((c, c), np.float32))]
    for h in _levels(c):
        p = np.zeros((c, c), np.float32)
        for t in range(c):
            mid = (t // (2 * h)) * 2 * h + h
            if t % (2 * h) >= h:
                p[t, mid:t + 1] = 1.0
            else:
                p[t, t + 1:mid] = 1.0
        blocks.append(p)
    return np.concatenate(blocks, axis=0)


def _gated_head_out(o, gate, w):
    o = o * lax.rsqrt(jnp.mean(o * o, axis=-1, keepdims=True) + EPS) * w
    return o * _silu(gate)


def _stack_heads(x, nh, w):
    return jnp.concatenate([x[:, i * w:(i + 1) * w] for i in range(nh)], axis=0)


def _diag_blocks(y, nh, c, w):
    return jnp.concatenate([y[i * c:(i + 1) * c, i * w:(i + 1) * w] for i in range(nh)], axis=0)


def _block_cols(v, nh, c):
    head = lax.broadcasted_iota(jnp.int32, (nh * c, 1), 0) >> (c.bit_length() - 1)
    return jnp.concatenate([jnp.where(head == i, v, 0.0) for i in range(nh)], axis=-1)


def _row_to_col(row):
    n = row.shape[1]
    eye = (lax.broadcasted_iota(jnp.int32, (n, n), 0) == lax.broadcasted_iota(jnp.int32, (n, n), 1))
    return jnp.sum(jnp.where(eye, row, 0.0), axis=1, keepdims=True)


def _linear_attn_block(q, k, v, g, s_cat, p_ref, c, nh, dv):
    dk = q.shape[1] // nh
    r = nh * c
    z = _sel_dot(p_ref[...], g)
    b = z[0:c]
    row = lax.broadcasted_iota(jnp.int32, (c, 1), 0)
    ri = lax.broadcasted_iota(jnp.int32, (r, r), 0)
    ci = lax.broadcasted_iota(jnp.int32, (r, r), 1)
    qk = _stack_heads(q * k, nh, dk)
    attn = jnp.where(ri == ci, jnp.sum(qk, axis=-1, keepdims=True), 0.0)
    for li, h in enumerate(_levels(c)):
        e = jnp.exp(z[(li + 1) * c:(li + 2) * c])
        lower = (row & (2 * h - 1)) >= h
        ql = _stack_heads(jnp.where(lower, q * e, 0.0), nh, dk).astype(BF16)
        kl = _stack_heads(jnp.where(lower, 0.0, k * e), nh, dk).astype(BF16)
        sh = (2 * h).bit_length() - 1
        attn = attn + jnp.where((ri >> sh) == (ci >> sh), _dot_nt(ql, kl), 0.0)
    vs = _stack_heads(v, nh, dv)
    qe = _stack_heads(q * jnp.exp(b), nh, dk)
    o = _diag_blocks(_mm(qe, s_cat), nh, c, dv) + _mm(attn, vs)
    b_last = b[c - 1:c]
    kt = _stack_heads(k * jnp.exp(b_last - b), nh, dk).astype(BF16)
    upd = _dot_tn(kt, _block_cols(vs, nh, c).astype(BF16))
    el = jnp.exp(b_last)
    dec = jnp.concatenate(
        [jnp.broadcast_to(_row_to_col(el[:, i * dk:(i + 1) * dk]), (dk, dv)) for i in range(nh)], axis=-1)
    return o, dec * s_cat + upd


def _gla_kernel(q_ref, k_ref, v_ref, gg_ref, sm_ref, wg_ref, bg_ref, nw_ref, p_ref, s0_ref,
                o_ref, s_ref, *, c):
    ci = pl.program_id(1)

    @pl.when(ci == 0)
    def _():
        s_ref[...] = s0_ref[...]

    x = _dot_sel(sm_ref[...], wg_ref[...]) + bg_ref[...]
    g = _log_sigmoid(x) * (1.0 / GLA_GATE_NORM)
    q = q_ref[...] * (GLA_DK ** -0.5)
    gate = gg_ref[...]
    s_cat = jnp.concatenate([s_ref[0, hh] for hh in range(GLA_H)], axis=-1)
    o, s_new = _linear_attn_block(q, k_ref[...], v_ref[...], g, s_cat, p_ref, c, GLA_H, GLA_DV)
    for hh in range(GLA_H):
        vs = slice(hh * GLA_DV, (hh + 1) * GLA_DV)
        s_ref[0, hh] = s_new[:, vs]
        o_ref[:, vs] = _gated_head_out(o[hh * c:(hh + 1) * c], gate[:, vs],
                                       nw_ref[...]).astype(o_ref.dtype)


def gla_mixer(proj, small, wg_pad, bg, norm_w, state, b, t, c):
    m = proj.shape[0]
    nc = t // c
    kw = GLA_H * GLA_DK
    vw = GLA_H * GLA_DV
    p = jnp.asarray(_prefix_matrix(c), BF16)
    row = lambda bb, cc: bb * nc + cc
    kern = functools.partial(_gla_kernel, c=c)
    return pl.pallas_call(
        kern,
        grid=(b, nc),
        in_specs=[pl.BlockSpec((c, kw), lambda bb, cc: (row(bb, cc), 0)),
                  pl.BlockSpec((c, kw), lambda bb, cc: (row(bb, cc), 1)),
                  pl.BlockSpec((c, vw), lambda bb, cc: (row(bb, cc), 1)),
                  pl.BlockSpec((c, vw), lambda bb, cc: (row(bb, cc), 2)),
                  pl.BlockSpec((c, LANE), lambda bb, cc: (row(bb, cc), 0)),
                  pl.BlockSpec((LANE, kw), lambda bb, cc: (0, 0)),
                  pl.BlockSpec((1, kw), lambda bb, cc: (0, 0)),
                  pl.BlockSpec((1, GLA_DV), lambda bb, cc: (0, 0)),
                  pl.BlockSpec(p.shape, lambda bb, cc: (0, 0)),
                  pl.BlockSpec((1, GLA_H, GLA_DK, GLA_DV), lambda bb, cc: (bb, 0, 0, 0))],
        out_specs=[pl.BlockSpec((c, vw), lambda bb, cc: (row(bb, cc), 0)),
                   pl.BlockSpec((1, GLA_H, GLA_DK, GLA_DV), lambda bb, cc: (bb, 0, 0, 0))],
        out_shape=[jax.ShapeDtypeStruct((m, vw), BF16),
                   jax.ShapeDtypeStruct((b, GLA_H, GLA_DK, GLA_DV), F32)],
        compiler_params=_cparams(("parallel", "arbitrary")),
        name="gla_mixer",
    )(proj, proj, proj, proj, small, wg_pad, bg.reshape(1, -1), norm_w.reshape(1, -1), p, state)


HG_HB = 4


def _hgrn_kernel(q_ref, f_ref, i_ref, g_ref, lb_ref, nw_ref, p_ref, s0_ref,
                 o_ref, s_ref, *, c):
    ci = pl.program_id(2)

    @pl.when(ci == 0)
    def _():
        s_ref[...] = s0_ref[...]

    lb = lb_ref[...]
    log_lb = jnp.log(jnp.maximum(lb, TINY))
    zf = f_ref[...]
    a = log_lb
    cc = jnp.log1p(-lb) + _log_sigmoid(zf)
    log_f = jnp.maximum(a, cc) + jnp.log1p(jnp.exp(-jnp.abs(a - cc)))
    key = (1.0 - lb) * jax.nn.sigmoid(-zf)
    q = _silu(q_ref[...])
    v = i_ref[...]
    gate = g_ref[...]
    s_cat = jnp.concatenate([s_ref[0, hh] for hh in range(HG_HB)], axis=-1)
    o, s_new = _linear_attn_block(q, key, v, log_f, s_cat, p_ref, c, HG_HB, HG_DV)
    for hh in range(HG_HB):
        sl = slice(hh * HG_DV, (hh + 1) * HG_DV)
        s_ref[0, hh] = s_new[:, sl]
        o_ref[:, sl] = _gated_head_out(o[hh * c:(hh + 1) * c], gate[:, sl],
                                       nw_ref[...]).astype(o_ref.dtype)


def hgrn_mixer(proj, lb, norm_w, state, b, t, c):
    col0 = 0
    m = proj.shape[0]
    nc = t // c
    ng = HG_H // HG_HB
    w = HG_HB * HG_DK
    p = jnp.asarray(_prefix_matrix(c), BF16)
    row = lambda bb, g, cc: bb * nc + cc
    kern = functools.partial(_hgrn_kernel, c=c)
    return pl.pallas_call(
        kern,
        grid=(b, ng, nc),
        in_specs=[pl.BlockSpec((c, w), lambda bb, g, cc: (row(bb, g, cc), col0 + g)),
                  pl.BlockSpec((c, w), lambda bb, g, cc: (row(bb, g, cc), col0 + ng + g)),
                  pl.BlockSpec((c, w), lambda bb, g, cc: (row(bb, g, cc), col0 + 2 * ng + g)),
                  pl.BlockSpec((c, w), lambda bb, g, cc: (row(bb, g, cc), col0 + 3 * ng + g)),
                  pl.BlockSpec((1, w), lambda bb, g, cc: (0, g)),
                  pl.BlockSpec((1, HG_DV), lambda bb, g, cc: (0, 0)),
                  pl.BlockSpec(p.shape, lambda bb, g, cc: (0, 0)),
                  pl.BlockSpec((1, HG_HB, HG_DK, HG_DV), lambda bb, g, cc: (bb, g, 0, 0))],
        out_specs=[pl.BlockSpec((c, w), lambda bb, g, cc: (row(bb, g, cc), g)),
                   pl.BlockSpec((1, HG_HB, HG_DK, HG_DV), lambda bb, g, cc: (bb, g, 0, 0))],
        out_shape=[jax.ShapeDtypeStruct((m, HG_H * HG_DV), BF16),
                   jax.ShapeDtypeStruct((b, HG_H, HG_DK, HG_DV), F32)],
        compiler_params=_cparams(("parallel", "parallel", "arbitrary")),
        name="hgrn_mixer",
    )(proj, proj, proj, proj, lb.reshape(1, -1), norm_w.reshape(1, -1), p, state)


GDN_HB = 4
GDN_GS = 4


def _each(f, *lists):
    return [f(*args) for args in zip(*lists)]


def _unit_lower_inverse(ns, r, c):
    ri = lax.broadcasted_iota(jnp.int32, (r, r), 0)
    ci = lax.broadcasted_iota(jnp.int32, (r, r), 1)
    eye = (ri == ci).astype(F32)
    d = [jnp.where((ri >> 4) == (ci >> 4), n, 0.0) for n in ns]
    x = [eye - d_ for d_ in d]
    p = d
    for _ in range(3):
        p = _each(_mm, p, p)
        x = _each(lambda x_, p_: x_ + _mm(x_, p_), x, p)
    size = 16
    while size < c:
        sh = size.bit_length() - 1
        mask = ((ri >> (sh + 1)) == (ci >> (sh + 1))) & ((ri >> sh) > (ci >> sh))
        lx = _each(lambda n, x_: _mm(jnp.where(mask, n, 0.0), x_), ns, x)
        x = _each(lambda x_, lx_: x_ - _mm(x_, lx_), x, lx)
        size *= 2
    return x


def _gdn_blocks(q, k, v, beta, b_c, b_last, s_cat, c, nh):
    r = nh * c
    ri = lax.broadcasted_iota(jnp.int32, (r, r), 0)
    ci = lax.broadcasted_iota(jnp.int32, (r, r), 1)
    shc = c.bit_length() - 1
    causal = ((ri >> shc) == (ci >> shc)) & (ri >= ci)

    def decay_of(bc):
        b_r = jnp.sum(jnp.where(ri == ci, bc, 0.0), axis=0, keepdims=True)
        return jnp.where(causal, jnp.exp(jnp.minimum(bc - b_r, 0.0)), 0.0)

    decay = _each(decay_of, b_c)
    kb = [k_.astype(BF16) for k_ in k]
    kq = _each(lambda kb_, q_: _dot_nt(jnp.concatenate([kb_, q_.astype(BF16)], axis=0), kb_), kb, q)
    n = _each(lambda be, kq_, de: jnp.where(ri > ci, be * kq_[:r] * de, 0.0), beta, kq, decay)
    tinv = _unit_lower_inverse(n, r, c)
    eb = [jnp.exp(bc) for bc in b_c]
    sol = _each(lambda t_, v_, k_, be, e_: _mm(t_, jnp.concatenate([v_ * be, k_ * (be * e_)], axis=-1)),
                tinv, v, k, beta, eb)
    ws_qs = _each(lambda so, q_, e_, s_: _mm(jnp.concatenate([so[:, GDN_DV:], q_ * e_], axis=0), s_),
                  sol, q, eb, s_cat)
    v_new = _each(lambda so, wq: so[:, :GDN_DV] - _diag_blocks(wq[:r], nh, c, GDN_DV), sol, ws_qs)
    o = _each(lambda wq, kq_, de, vn: _diag_blocks(wq[r:], nh, c, GDN_DV) + _mm(kq_[r:] * de, vn),
              ws_qs, kq, decay, v_new)
    upd = _each(lambda k_, bl, bc, vn: _dot_tn((k_ * jnp.exp(bl - bc)).astype(BF16),
                                               _block_cols(vn, nh, c).astype(BF16)),
                k, b_last, b_c, v_new)

    def decayed(bl, s_, up):
        el = jnp.exp(bl)
        dec = jnp.concatenate(
            [jnp.broadcast_to(el[i * c:i * c + 1], (1, GDN_DV)) for i in range(nh)], axis=-1)
        return dec * s_ + up

    return o, _each(decayed, b_last, s_cat, upd)


def _gdn_kernel(q_ref, k_ref, v_ref, z_ref, gc_ref, wq_ref, wk_ref, wv_ref,
                pc_ref, nw_ref, tril_ref, s0_ref, cq0_ref, ck0_ref, cv0_ref,
                o_ref, s_ref, cq_ref, ck_ref, cv_ref,
                xq, xk, xv, *, c, nc):
    ci = pl.program_id(2)
    last = GDN_CONV - 1

    @pl.when(ci == 0)
    def _():
        s_ref[...] = s0_ref[...]

    def conv(x_ref, w_ref, c0_ref, c_ref, scr):
        @pl.when(ci == 0)
        def _():
            scr[8 - last:8, :] = c0_ref[0]

        scr[8:8 + c, :] = x_ref[...]
        y = scr[8:8 + c, :] * w_ref[last:last + 1, :]
        for j in range(last):
            y = y + scr[8 - last + j:8 - last + j + c, :] * w_ref[j:j + 1, :]
        tail = scr[8 + c - last:8 + c, :]
        scr[8 - last:8, :] = tail

        @pl.when(ci == nc - 1)
        def _():
            c_ref[0] = tail

        return _silu(y)

    qa = conv(q_ref, wq_ref, cq0_ref, cq_ref, xq)
    ka = conv(k_ref, wk_ref, ck0_ref, ck_ref, xk)
    va = conv(v_ref, wv_ref, cv0_ref, cv_ref, xv)

    nh = GDN_HB
    gw = nh * GDN_DK
    q, k, v, beta, b_c, b_last, s_cat = [], [], [], [], [], [], []
    for gi in range(GDN_GS):
        gcol = gc_ref[gi]
        pc = pc_ref[gi]
        beta_all = jax.nn.sigmoid(gcol)
        g_c = pc[0:1, :] * _softplus(gcol + pc[1:2, :])
        b_col = _sel_dot(tril_ref[...], g_c)
        beta.append(jnp.concatenate([beta_all[:, hh:hh + 1] for hh in range(nh)], axis=0))
        b_c.append(jnp.concatenate([b_col[:, nh + hh:nh + hh + 1] for hh in range(nh)], axis=0))
        b_last.append(jnp.concatenate(
            [jnp.broadcast_to(b_col[c - 1:c, nh + hh:nh + hh + 1], (c, 1)) for hh in range(nh)],
            axis=0))
        gs = slice(gi * gw, (gi + 1) * gw)
        qg = _stack_heads(qa[:, gs], nh, GDN_DK)
        kg = _stack_heads(ka[:, gs], nh, GDN_DK)
        q.append(qg * lax.rsqrt(jnp.sum(qg * qg, axis=-1, keepdims=True) + EPS) * (GDN_DK ** -0.5))
        k.append(kg * lax.rsqrt(jnp.sum(kg * kg, axis=-1, keepdims=True) + EPS))
        v.append(_stack_heads(va[:, gs], nh, GDN_DV))
        s_cat.append(jnp.concatenate([s_ref[0, gi * nh + hh] for hh in range(nh)], axis=-1))
    o, s_new = _gdn_blocks(q, k, v, beta, b_c, b_last, s_cat, c, nh)
    z = z_ref[...]
    for gi in range(GDN_GS):
        for hh in range(nh):
            sl = slice(hh * GDN_DV, (hh + 1) * GDN_DV)
            zl = slice(gi * gw + hh * GDN_DV, gi * gw + (hh + 1) * GDN_DV)
            s_ref[0, gi * nh + hh] = s_new[gi][:, sl]
            o_ref[:, zl] = _gated_head_out(o[gi][hh * c:(hh + 1) * c], z[:, zl],
                                           nw_ref[...]).astype(o_ref.dtype)


def gdn_mixer(proj, gates_c, conv_w, par_c, norm_w, state, cache, b, t, c):
    m = proj.shape[0]
    nc = t // c
    hs = GDN_HB * GDN_GS
    ng = GDN_H // hs
    w = hs * GDN_DK
    col0 = 0
    last = GDN_CONV - 1
    tril = jnp.asarray(np.tril(np.ones((c, c), np.float32)), BF16)
    row = lambda bb, g, cc: bb * nc + cc
    kern = functools.partial(_gdn_kernel, c=c, nc=nc)
    big = lambda off: pl.BlockSpec((c, w), lambda bb, g, cc: (row(bb, g, cc), col0 + off * ng + g))
    cw = lambda off: pl.BlockSpec((GDN_CONV, w), lambda bb, g, cc: (0, off * ng + g))
    c0 = lambda off: pl.BlockSpec((1, last, w), lambda bb, g, cc: (bb, 0, off * ng + g))
    cout = pl.BlockSpec((1, last, w), lambda bb, g, cc: (bb, 0, g))
    outs = pl.pallas_call(
        kern,
        grid=(b, ng, nc),
        in_specs=[big(0), big(1), big(2), big(3),
                  pl.BlockSpec((GDN_GS, c, LANE), lambda bb, g, cc: (g, row(bb, g, cc), 0)),
                  cw(0), cw(1), cw(2),
                  pl.BlockSpec((GDN_GS, 2, LANE), lambda bb, g, cc: (g, 0, 0)),
                  pl.BlockSpec((1, GDN_DV), lambda bb, g, cc: (0, 0)),
                  pl.BlockSpec((c, c), lambda bb, g, cc: (0, 0)),
                  pl.BlockSpec((1, hs, GDN_DK, GDN_DV), lambda bb, g, cc: (bb, g, 0, 0)),
                  c0(0), c0(1), c0(2)],
        out_specs=[pl.BlockSpec((c, w), lambda bb, g, cc: (row(bb, g, cc), g)),
                   pl.BlockSpec((1, hs, GDN_DK, GDN_DV), lambda bb, g, cc: (bb, g, 0, 0)),
                   cout, cout, cout],
        out_shape=[jax.ShapeDtypeStruct((m, GDN_H * GDN_DV), BF16),
                   jax.ShapeDtypeStruct((b, GDN_H, GDN_DK, GDN_DV), F32),
                   jax.ShapeDtypeStruct((b, last, GDN_H * GDN_DK), F32),
                   jax.ShapeDtypeStruct((b, last, GDN_H * GDN_DK), F32),
                   jax.ShapeDtypeStruct((b, last, GDN_H * GDN_DV), F32)],
        scratch_shapes=[pltpu.VMEM((8 + c, w), F32), pltpu.VMEM((8 + c, w), F32),
                        pltpu.VMEM((8 + c, w), F32)],
        compiler_params=_cparams(("parallel", "parallel", "arbitrary")),
        name="gdn_mixer",
    )(proj, proj, proj, proj, gates_c, conv_w, conv_w, conv_w, par_c,
      norm_w.reshape(1, -1), tril, state, cache, cache, cache)
    o, s, cq, ck, cv = outs
    return o, s, jnp.concatenate([cq, ck, cv], axis=-1)


def _row_tile(m, cap):
    tm = min(m, cap)
    while m % tm:
        tm //= 2
    return tm


def _group_gates(small):
    m = small.shape[0]
    ng = GDN_H // GDN_HB
    db = small[:, GLA_RANK:GLA_RANK + GDN_H].reshape(m, ng, GDN_HB)
    da = small[:, GLA_RANK + GDN_H:GLA_RANK + 2 * GDN_H].reshape(m, ng, GDN_HB)
    g = jnp.concatenate([db, da, jnp.zeros((m, ng, LANE - 2 * GDN_HB), F32)], axis=-1)
    return jnp.transpose(g, (1, 0, 2))


def _group_params(a_log, dt_bias):
    ng = GDN_H // GDN_HB
    na = -jnp.exp(a_log.astype(F32)).reshape(ng, GDN_HB)
    dt = dt_bias.astype(F32).reshape(ng, GDN_HB)
    pad = jnp.zeros((ng, GDN_HB), F32)
    tail = jnp.zeros((ng, LANE - 2 * GDN_HB), F32)
    return jnp.stack([jnp.concatenate([pad, na, tail], axis=-1),
                      jnp.concatenate([pad, dt, tail], axis=-1)], axis=1)


COL_GLA = 0
N_GLA = 2 * GLA_H * GLA_DK + 2 * GLA_H * GLA_DV
COL_LR = COL_GLA + N_GLA
COL_GDN = COL_LR + GLA_RANK
N_GDN = 2 * GDN_H * GDN_DK + 2 * GDN_H * GDN_DV
COL_BA = COL_GDN + N_GDN
COL_HG = COL_BA + 2 * GDN_H
N_HG = 2 * HG_H * HG_DK + 2 * HG_H * HG_DV

COL_TILE = 256
DOWN_TILES = (1024, 512)


def _prep_weights(w_in, w_gla_gate, w_down):
    w_small = jnp.concatenate([w_in[:, :, COL_LR:COL_GDN], w_in[:, :, COL_BA:COL_HG]], axis=-1)
    w_small = jnp.pad(w_small, ((0, 0), (0, 0), (0, LANE - w_small.shape[-1]))).astype(BF16)
    wg_pad = jnp.pad(w_gla_gate.astype(BF16), ((0, 0), (0, LANE - GLA_RANK), (0, 0)))
    return w_small, wg_pad, w_down.astype(BF16)


def _trunk(x, pe, st_gla, st_gdn, cb_gdn, st_hg, cb_ffn, wts, prm, lower_bounds, norm_final):
    (w_in, w_small, wg_pad, w_out, w_up, w_down, w_pg, w_pp) = wts
    (norm_mix, b_gla_gate, gla_norm, w_gdn_conv, gdn_a_log, gdn_dt_bias, gdn_norm, hgrn_norm,
     norm_ffn, w_ffn_conv, norm_ple) = prm
    b, t, d = x.shape
    depth = w_in.shape[0]
    m = b * t
    c = min(MAX_CHUNK, t)
    tm = _row_tile(m, ROW_TILE)
    tr = _row_tile(t, 256)
    tn = COL_TILE
    f = w_down.shape[1]
    h = x.reshape(m, d)
    n_gla, n_gdn, n_gconv, n_hg, n_fconv = [], [], [], [], []
    for li in range(depth):
        xn = rmsnorm(h, norm_mix[li], BF16, _row_tile(m, 256))
        p_gla = matmul(xn, w_in, li, F32, tm, tn, COL_GLA, N_GLA)
        p_gdn = matmul(xn, w_in, li, F32, tm, tn, COL_GDN, N_GDN)
        p_hg = matmul(xn, w_in, li, F32, tm, tn, COL_HG, N_HG)
        small = matmul(xn, w_small, li, F32, tm, LANE)

        o_gla, s_gla = gla_mixer(p_gla, small, wg_pad[li], b_gla_gate[li], gla_norm[li],
                                 st_gla[li], b, t, c)
        o_gdn, s_gdn, b_gdn = gdn_mixer(p_gdn, _group_gates(small), w_gdn_conv[li],
                                        _group_params(gdn_a_log[li], gdn_dt_bias[li]), gdn_norm[li],
                                        st_gdn[li], cb_gdn[li], b, t, c)
        o_hg, s_hg = hgrn_mixer(p_hg, lower_bounds[li], hgrn_norm[li], st_hg[li], b, t, c)
        mix = jnp.concatenate([o_gla, o_gdn, o_hg], axis=-1)
        h = matmul_residual(mix, w_out, li, h, tm, tn, mix.shape[1])

        xn = rmsnorm(h, norm_ffn[li], BF16, _row_tile(m, 256))
        if t % tm == 0:
            hid, cg, cv = ffn_up_fused(xn, w_up, li, w_ffn_conv[li], cb_ffn[li], b, t, tm, tn)
        else:
            up = matmul(xn, w_up, li, F32, tm, tn)
            hid, cg, cv = ffn_conv_act(up, w_ffn_conv[li], cb_ffn[li], b, t, tr, f // 2)
        b_ffn = jnp.concatenate([cg, cv], axis=-1)
        h = matmul_residual(hid, w_down, li, h, _row_tile(m, DOWN_TILES[0]), DOWN_TILES[1], f // 2)

        xn = rmsnorm(h, norm_ple[li], BF16, _row_tile(m, 256))
        h = ple_update(xn, w_pg, pe[li].reshape(m, -1).astype(BF16), w_pp, li, h, tm, tn)

        n_gla.append(s_gla)
        n_gdn.append(s_gdn)
        n_gconv.append(b_gdn)
        n_hg.append(s_hg)
        n_fconv.append(b_ffn)
    y = rmsnorm(h, norm_final, F32, _row_tile(m, 256)).reshape(b, t, d)
    return (y, jnp.stack(n_gla), jnp.stack(n_gdn), jnp.stack(n_gconv),
            jnp.stack(n_hg), jnp.stack(n_fconv))


def kernel(x_prompt, x_sample, p_prompt, p_sample, state_gla, state_gdn, cache_gdn_conv, state_hgrn, cache_ffn_conv, norm_mix, w_in, w_gla_gate, b_gla_gate, gla_norm, w_gdn_conv, gdn_a_log, gdn_dt_bias, gdn_norm, hgrn_lb, hgrn_norm, w_out, norm_ffn, w_up, w_ffn_conv, w_down, norm_ple, w_ple_gate, w_ple_proj, norm_final):
    depth = w_in.shape[0]
    bp = x_prompt.shape[0]
    w_small, wg_pad, w_down_b = _prep_weights(w_in, w_gla_gate, w_down)
    wts = (w_in, w_small, wg_pad, w_out, w_up, w_down_b, w_ple_gate, w_ple_proj)
    prm = (norm_mix, b_gla_gate, gla_norm, w_gdn_conv, gdn_a_log, gdn_dt_bias, gdn_norm,
           hgrn_norm, norm_ffn, w_ffn_conv, norm_ple)
    sm = jax.nn.softmax(hgrn_lb.astype(F32), axis=0)
    lower_bounds = jnp.cumsum(sm, axis=0) - sm[0]
    zeros = lambda a: jnp.zeros((depth, bp) + a.shape[2:], F32)
    prompt = _trunk(x_prompt, p_prompt, zeros(state_gla), zeros(state_gdn), zeros(cache_gdn_conv),
                    zeros(state_hgrn), zeros(cache_ffn_conv), wts, prm, lower_bounds, norm_final)
    sample = _trunk(x_sample, p_sample, state_gla, state_gdn, cache_gdn_conv, state_hgrn,
                    cache_ffn_conv, wts, prm, lower_bounds, norm_final)
    return (prompt[0], sample[0]) + prompt[1:] + sample[1:]
```

```python
import functools

import numpy as np
import jax
import jax.numpy as jnp
from jax import lax
from jax.experimental import pallas as pl
from jax.experimental.pallas import tpu as pltpu

F32 = jnp.float32
BF16 = jnp.bfloat16

EPS = 1e-6
TINY = 1e-30
GLA_GATE_NORM = 16.0
GLA_H, GLA_DK, GLA_DV, GLA_RANK = 4, 128, 256, 16
GDN_H, GDN_DK, GDN_DV, GDN_CONV = 16, 128, 128, 4
HG_H, HG_DK, HG_DV = 8, 128, 128
FFN_CONV = 3
MAX_CHUNK = 64
ROW_TILE = 2048

LANE = 128
VMEM_LIMIT = 56 * 1024 * 1024


def _cparams(sem):
    return pltpu.CompilerParams(dimension_semantics=sem, vmem_limit_bytes=VMEM_LIMIT)


def _dot(a, b):
    return jnp.dot(a, b, preferred_element_type=F32)


def _dot_nt(a, b):
    return lax.dot_general(a, b, (((1,), (1,)), ((), ())), preferred_element_type=F32)


def _dot_tn(a, b):
    return lax.dot_general(a, b, (((0,), (0,)), ((), ())), preferred_element_type=F32)


def _mm(a, b):
    return _dot(a.astype(BF16), b.astype(BF16))


def _split3(x):
    hi = x.astype(BF16)
    r = x - hi.astype(F32)
    mid = r.astype(BF16)
    lo = (r - mid.astype(F32)).astype(BF16)
    return hi, mid, lo


def _sel_dot(p, x):
    hi, mid, lo = _split3(x)
    return _dot(p, hi) + _dot(p, mid) + _dot(p, lo)


def _dot_sel(x, p):
    hi, mid, lo = _split3(x)
    return _dot(hi, p) + _dot(mid, p) + _dot(lo, p)


def _softplus(x):
    return jnp.maximum(x, 0.0) + jnp.log1p(jnp.exp(-jnp.abs(x)))


def _log_sigmoid(x):
    return -_softplus(-x)


def _silu(x):
    return x * jax.nn.sigmoid(x)


def _rmsnorm_kernel(x_ref, w_ref, o_ref):
    x = x_ref[...]
    y = x * lax.rsqrt(jnp.mean(x * x, axis=-1, keepdims=True) + EPS)
    o_ref[...] = (y * w_ref[...]).astype(o_ref.dtype)


def rmsnorm(x, w, out_dtype, tm):
    m, d = x.shape
    return pl.pallas_call(
        _rmsnorm_kernel,
        grid=(m // tm,),
        in_specs=[pl.BlockSpec((tm, d), lambda i: (i, 0)),
                  pl.BlockSpec((1, d), lambda i: (0, 0))],
        out_specs=pl.BlockSpec((tm, d), lambda i: (i, 0)),
        out_shape=jax.ShapeDtypeStruct((m, d), out_dtype),
        compiler_params=_cparams(("parallel",)),
        name="rmsnorm",
    )(x, w.reshape(1, d))


SINGLE_BUFFER_BYTES = 16 * 1024 * 1024


def _act_spec(shape, index_map, itemsize=2):
    if shape[0] * shape[1] * itemsize >= SINGLE_BUFFER_BYTES:
        return pl.BlockSpec(shape, index_map, pipeline_mode=pl.Buffered(1))
    return pl.BlockSpec(shape, index_map)


def _mm_kernel(x_ref, w_ref, o_ref):
    o_ref[...] = _dot(x_ref[...], w_ref[...].astype(BF16)).astype(o_ref.dtype)


def _mm_shift_kernel(x_ref, wa_ref, wb_ref, o_ref, *, shift):
    tn = wa_ref.shape[1]
    w = jnp.concatenate([wa_ref[...], wb_ref[...]], axis=1)[:, shift:shift + tn]
    o_ref[...] = _dot(x_ref[...], w.astype(BF16)).astype(o_ref.dtype)


def matmul(x, w, li, out_dtype, tm, tn, col0=0, ncols=None):
    m, k = x.shape
    ncols = w.shape[2] if ncols is None else ncols
    shift = col0 % LANE
    base = (col0 - shift) // tn
    assert (col0 - shift) % tn == 0 and ncols % tn == 0
    x_spec = _act_spec((tm, k), lambda i, j: (i, 0))
    wa_spec = pl.BlockSpec((None, k, tn), lambda i, j: (li, 0, base + j))
    common = dict(
        grid=(m // tm, ncols // tn),
        out_specs=pl.BlockSpec((tm, tn), lambda i, j: (i, j)),
        out_shape=jax.ShapeDtypeStruct((m, ncols), out_dtype),
        compiler_params=_cparams(("parallel", "arbitrary")),
        name="matmul",
    )
    if shift == 0:
        return pl.pallas_call(_mm_kernel, in_specs=[x_spec, wa_spec], **common)(x, w)
    lanes_per_tile = tn // LANE
    wb_spec = pl.BlockSpec((None, k, LANE), lambda i, j: (li, 0, (base + j + 1) * lanes_per_tile))
    return pl.pallas_call(functools.partial(_mm_shift_kernel, shift=shift),
                          in_specs=[x_spec, wa_spec, wb_spec], **common)(x, w, w)


def _gate_proj_kernel(x_ref, wa_ref, wb_ref, o_ref, *, n_a, off_b, n_b):
    x = x_ref[...]
    a = _dot(x, wa_ref[...].astype(BF16))
    b = _dot(x, wb_ref[...].astype(BF16))
    lane = lax.broadcasted_iota(jnp.int32, a.shape, 1)
    o_ref[...] = jnp.where(lane < n_a, a, jnp.where((lane >= off_b) & (lane < off_b + n_b), b, 0.0))


def gate_proj(x, w, li, tm, col_a, n_a, col_b, n_b):
    m, k = x.shape
    off_b = col_b % LANE
    assert col_a % LANE == 0 and n_a <= off_b and off_b + n_b <= LANE
    kern = functools.partial(_gate_proj_kernel, n_a=n_a, off_b=off_b, n_b=n_b)
    return pl.pallas_call(
        kern,
        grid=(m // tm,),
        in_specs=[pl.BlockSpec((tm, k), lambda i: (i, 0)),
                  pl.BlockSpec((None, k, LANE), lambda i: (li, 0, col_a // LANE)),
                  pl.BlockSpec((None, k, LANE), lambda i: (li, 0, col_b // LANE))],
        out_specs=pl.BlockSpec((tm, LANE), lambda i: (i, 0)),
        out_shape=jax.ShapeDtypeStruct((m, LANE), F32),
        compiler_params=_cparams(("parallel",)),
        name="gate_proj",
    )(x, w, w)


def _mm_res_kernel(x_ref, w_ref, r_ref, o_ref):
    k = pl.program_id(2)

    @pl.when(k == 0)
    def _():
        o_ref[...] = r_ref[...]

    o_ref[...] += _dot(x_ref[...], w_ref[...].astype(BF16))


def matmul_residual(x, w, li, res, tm, tn, tk):
    m, k = x.shape
    n = w.shape[2]
    return pl.pallas_call(
        _mm_res_kernel,
        grid=(m // tm, n // tn, k // tk),
        in_specs=[_act_spec((tm, tk), lambda i, j, kk: (i, kk)) if k == tk
                  else pl.BlockSpec((tm, tk), lambda i, j, kk: (i, kk)),
                  pl.BlockSpec((None, tk, tn), lambda i, j, kk: (li, kk, j)),
                  pl.BlockSpec((tm, tn), lambda i, j, kk: (i, j))],
        out_specs=pl.BlockSpec((tm, tn), lambda i, j, kk: (i, j)),
        out_shape=jax.ShapeDtypeStruct((m, n), F32),
        compiler_params=_cparams(("parallel", "arbitrary", "arbitrary")),
        name="matmul_residual",
    )(x, w, res)


def _ple_kernel(x_ref, wg_ref, p_ref, wp_ref, r_ref, o_ref):
    gate = jax.nn.sigmoid(_dot(x_ref[...], wg_ref[...].astype(BF16)))
    o_ref[...] = r_ref[...] + gate * _dot(p_ref[...], wp_ref[...].astype(BF16))


def ple_update(xn, wg, pe, wp, li, res, tm, tn):
    m, k = xn.shape
    n = wg.shape[2]
    kp = pe.shape[1]
    return pl.pallas_call(
        _ple_kernel,
        grid=(m // tm, n // tn),
        in_specs=[_act_spec((tm, k), lambda i, j: (i, 0)),
                  pl.BlockSpec((None, k, tn), lambda i, j: (li, 0, j)),
                  pl.BlockSpec((tm, kp), lambda i, j: (i, 0)),
                  pl.BlockSpec((None, kp, tn), lambda i, j: (li, 0, j)),
                  pl.BlockSpec((tm, tn), lambda i, j: (i, j))],
        out_specs=pl.BlockSpec((tm, tn), lambda i, j: (i, j)),
        out_shape=jax.ShapeDtypeStruct((m, n), F32),
        compiler_params=_cparams(("parallel", "arbitrary")),
        name="ple_update",
    )(xn, wg, pe, wp, res)


def _ffn_act_kernel(g_ref, v_ref, wg_ref, wv_ref, cg0_ref, cv0_ref,
                    h_ref, cg_ref, cv_ref, sg, sv, *, tr, n_row_tiles):
    i = pl.program_id(2)
    last = FFN_CONV - 1

    def conv(x_ref, w_ref, c0_ref, c_ref, scr):
        @pl.when(i == 0)
        def _():
            scr[8 - last:8, :] = c0_ref[0]

        scr[8:8 + tr, :] = x_ref[...]
        y = scr[8:8 + tr, :] * w_ref[last:last + 1, :]
        for j in range(last):
            y = y + scr[8 - last + j:8 - last + j + tr, :] * w_ref[j:j + 1, :]
        tail = scr[8 + tr - last:8 + tr, :]
        scr[8 - last:8, :] = tail

        @pl.when(i == n_row_tiles - 1)
        def _():
            c_ref[0] = tail

        return y

    gate = conv(g_ref, wg_ref, cg0_ref, cg_ref, sg)
    val = conv(v_ref, wv_ref, cv0_ref, cv_ref, sv)
    h_ref[...] = (_silu(gate) * val).astype(h_ref.dtype)


def ffn_conv_act(up, w_conv, cache, b, t, tr, tc):
    m, f2 = up.shape
    f = f2 // 2
    nj = f // tc
    n_row_tiles = t // tr
    last = FFN_CONV - 1
    kern = functools.partial(_ffn_act_kernel, tr=tr, n_row_tiles=n_row_tiles)
    return pl.pallas_call(
        kern,
        grid=(b, nj, n_row_tiles),
        in_specs=[pl.BlockSpec((tr, tc), lambda bb, j, i: (bb * n_row_tiles + i, j)),
                  pl.BlockSpec((tr, tc), lambda bb, j, i: (bb * n_row_tiles + i, nj + j)),
                  pl.BlockSpec((FFN_CONV, tc), lambda bb, j, i: (0, j)),
                  pl.BlockSpec((FFN_CONV, tc), lambda bb, j, i: (0, nj + j)),
                  pl.BlockSpec((1, last, tc), lambda bb, j, i: (bb, 0, j)),
                  pl.BlockSpec((1, last, tc), lambda bb, j, i: (bb, 0, nj + j))],
        out_specs=[pl.BlockSpec((tr, tc), lambda bb, j, i: (bb * n_row_tiles + i, j)),
                   pl.BlockSpec((1, last, tc), lambda bb, j, i: (bb, 0, j)),
                   pl.BlockSpec((1, last, tc), lambda bb, j, i: (bb, 0, j))],
        out_shape=[jax.ShapeDtypeStruct((m, f), BF16),
                   jax.ShapeDtypeStruct((b, last, f), F32),
                   jax.ShapeDtypeStruct((b, last, f), F32)],
        scratch_shapes=[pltpu.VMEM((8 + tr, tc), F32), pltpu.VMEM((8 + tr, tc), F32)],
        compiler_params=_cparams(("parallel", "parallel", "arbitrary")),
        name="ffn_conv_act",
    )(up, up, w_conv, w_conv, cache, cache)


def _ffn_up_kernel(x_ref, wg_ref, wv_ref, cwg_ref, cwv_ref, cg0_ref, cv0_ref,
                   h_ref, cg_ref, cv_ref, *, tm):
    last = FFN_CONV - 1
    x = x_ref[...]
    row = lax.broadcasted_iota(jnp.int32, (8, 1), 0)

    def conv(w_ref, cw_ref, c0_ref, c_ref):
        up = _dot(x, w_ref[...].astype(BF16))
        c_ref[0] = up[tm - last:tm, :]
        c0 = c0_ref[0]
        y = up * cw_ref[last:last + 1, :]
        for s in range(1, last + 1):
            sh = pltpu.roll(up, s, axis=0)
            head = sh[0:8]
            for r in range(s):
                head = jnp.where(row == r, c0[last - s + r:last - s + r + 1, :], head)
            y = y + jnp.concatenate([head, sh[8:]], axis=0) * cw_ref[last - s:last - s + 1, :]
        return y

    gate = conv(wg_ref, cwg_ref, cg0_ref, cg_ref)
    val = conv(wv_ref, cwv_ref, cv0_ref, cv_ref)
    h_ref[...] = (_silu(gate) * val).astype(h_ref.dtype)


def ffn_up_fused(xn, w_up, li, w_conv, cache, tn):
    m, d = xn.shape
    b = cache.shape[0]
    tm = m // b
    f = w_up.shape[2] // 2
    nj = f // tn
    last = FFN_CONV - 1
    kern = functools.partial(_ffn_up_kernel, tm=tm)
    cspec = lambda off: pl.BlockSpec((1, last, tn), lambda i, j: (i, 0, off + j))
    return pl.pallas_call(
        kern,
        grid=(b, nj),
        in_specs=[_act_spec((tm, d), lambda i, j: (i, 0)),
                  pl.BlockSpec((None, d, tn), lambda i, j: (li, 0, j)),
                  pl.BlockSpec((None, d, tn), lambda i, j: (li, 0, nj + j)),
                  pl.BlockSpec((FFN_CONV, tn), lambda i, j: (0, j)),
                  pl.BlockSpec((FFN_CONV, tn), lambda i, j: (0, nj + j)),
                  cspec(0), cspec(nj)],
        out_specs=[pl.BlockSpec((tm, tn), lambda i, j: (i, j)), cspec(0), cspec(0)],
        out_shape=[jax.ShapeDtypeStruct((m, f), BF16),
                   jax.ShapeDtypeStruct((b, last, f), F32),
                   jax.ShapeDtypeStruct((b, last, f), F32)],
        compiler_params=_cparams(("parallel", "arbitrary")),
        name="ffn_up_fused",
    )(xn, w_up, w_up, w_conv, w_conv, cache, cache)


def _levels(c):
    out, h = [], c // 2
    while h >= 1:
        out.append(h)
        h //= 2
    return out


def _prefix_matrix(c):
    blocks = [np.tril(np.ones((c, c), np.float32))]
    for h in _levels(c):
        p = np.zeros((c, c), np.float32)
        for t in range(c):
            mid = (t // (2 * h)) * 2 * h + h
            if t % (2 * h) >= h:
                p[t, mid:t + 1] = 1.0
            else:
                p[t, t + 1:mid] = 1.0
        blocks.append(p)
    return np.concatenate(blocks, axis=0)


def _gated_head_out(o, gate, w):
    o = o * lax.rsqrt(jnp.mean(o * o, axis=-1, keepdims=True) + EPS) * w
    return o * _silu(gate)


def _stack_heads(x, nh, w):
    return jnp.concatenate([x[:, i * w:(i + 1) * w] for i in range(nh)], axis=0)


def _diag_blocks(y, nh, c, w):
    return jnp.concatenate([y[i * c:(i + 1) * c, i * w:(i + 1) * w] for i in range(nh)], axis=0)


def _block_cols(v, nh, c):
    head = lax.broadcasted_iota(jnp.int32, (nh * c, 1), 0) >> (c.bit_length() - 1)
    return jnp.concatenate([jnp.where(head == i, v, 0.0) for i in range(nh)], axis=-1)


def _row_to_col(row):
    n = row.shape[1]
    eye = (lax.broadcasted_iota(jnp.int32, (n, n), 0) == lax.broadcasted_iota(jnp.int32, (n, n), 1))
    return jnp.sum(jnp.where(eye, row, 0.0), axis=1, keepdims=True)


def _linear_attn_block(q, k, v, g, s_cat, p_ref, c, nh, dv):
    dk = q.shape[1] // nh
    r = nh * c
    z = _sel_dot(p_ref[...], g)
    b = z[0:c]
    row = lax.broadcasted_iota(jnp.int32, (c, 1), 0)
    ri = lax.broadcasted_iota(jnp.int32, (r, r), 0)
    ci = lax.broadcasted_iota(jnp.int32, (r, r), 1)
    qk = _stack_heads(q * k, nh, dk)
    attn = jnp.where(ri == ci, jnp.sum(qk, axis=-1, keepdims=True), 0.0)
    for li, h in enumerate(_levels(c)):
        e = jnp.exp(z[(li + 1) * c:(li + 2) * c])
        lower = (row & (2 * h - 1)) >= h
        ql = _stack_heads(jnp.where(lower, q * e, 0.0), nh, dk).astype(BF16)
        kl = _stack_heads(jnp.where(lower, 0.0, k * e), nh, dk).astype(BF16)
        sh = (2 * h).bit_length() - 1
        attn = attn + jnp.where((ri >> sh) == (ci >> sh), _dot_nt(ql, kl), 0.0)
    vs = _stack_heads(v, nh, dv)
    qe = _stack_heads(q * jnp.exp(b), nh, dk)
    o = _diag_blocks(_mm(qe, s_cat), nh, c, dv) + _mm(attn, vs)
    b_last = b[c - 1:c]
    kt = _stack_heads(k * jnp.exp(b_last - b), nh, dk).astype(BF16)
    upd = _dot_tn(kt, _block_cols(vs, nh, c).astype(BF16))
    el = jnp.exp(b_last)
    dec = jnp.concatenate(
        [jnp.broadcast_to(_row_to_col(el[:, i * dk:(i + 1) * dk]), (dk, dv)) for i in range(nh)], axis=-1)
    return o, dec * s_cat + upd


def _gla_kernel(q_ref, k_ref, v_ref, gg_ref, sm_ref, wg_ref, bg_ref, nw_ref, p_ref, s0_ref,
                o_ref, s_ref, *, c):
    ci = pl.program_id(1)

    @pl.when(ci == 0)
    def _():
        s_ref[...] = s0_ref[...]

    x = _dot_sel(sm_ref[...], wg_ref[...]) + bg_ref[...]
    g = _log_sigmoid(x) * (1.0 / GLA_GATE_NORM)
    q = q_ref[...] * (GLA_DK ** -0.5)
    gate = gg_ref[...]
    s_cat = jnp.concatenate([s_ref[0, hh] for hh in range(GLA_H)], axis=-1)
    o, s_new = _linear_attn_block(q, k_ref[...], v_ref[...], g, s_cat, p_ref, c, GLA_H, GLA_DV)
    for hh in range(GLA_H):
        vs = slice(hh * GLA_DV, (hh + 1) * GLA_DV)
        s_ref[0, hh] = s_new[:, vs]
        o_ref[:, vs] = _gated_head_out(o[hh * c:(hh + 1) * c], gate[:, vs],
                                       nw_ref[...]).astype(o_ref.dtype)


def gla_mixer(proj, small, wg_pad, bg, norm_w, state, b, t, c):
    m = proj.shape[0]
    nc = t // c
    kw = GLA_H * GLA_DK
    vw = GLA_H * GLA_DV
    p = jnp.asarray(_prefix_matrix(c), BF16)
    row = lambda bb, cc: bb * nc + cc
    kern = functools.partial(_gla_kernel, c=c)
    return pl.pallas_call(
        kern,
        grid=(b, nc),
        in_specs=[pl.BlockSpec((c, kw), lambda bb, cc: (row(bb, cc), 0)),
                  pl.BlockSpec((c, kw), lambda bb, cc: (row(bb, cc), 1)),
                  pl.BlockSpec((c, vw), lambda bb, cc: (row(bb, cc), 1)),
                  pl.BlockSpec((c, vw), lambda bb, cc: (row(bb, cc), 2)),
                  pl.BlockSpec((c, LANE), lambda bb, cc: (row(bb, cc), 0)),
                  pl.BlockSpec((LANE, kw), lambda bb, cc: (0, 0)),
                  pl.BlockSpec((1, kw), lambda bb, cc: (0, 0)),
                  pl.BlockSpec((1, GLA_DV), lambda bb, cc: (0, 0)),
                  pl.BlockSpec(p.shape, lambda bb, cc: (0, 0)),
                  pl.BlockSpec((1, GLA_H, GLA_DK, GLA_DV), lambda bb, cc: (bb, 0, 0, 0))],
        out_specs=[pl.BlockSpec((c, vw), lambda bb, cc: (row(bb, cc), 0)),
                   pl.BlockSpec((1, GLA_H, GLA_DK, GLA_DV), lambda bb, cc: (bb, 0, 0, 0))],
        out_shape=[jax.ShapeDtypeStruct((m, vw), BF16),
                   jax.ShapeDtypeStruct((b, GLA_H, GLA_DK, GLA_DV), F32)],
        compiler_params=_cparams(("parallel", "arbitrary")),
        name="gla_mixer",
    )(proj, proj, proj, proj, small, wg_pad, bg.reshape(1, -1), norm_w.reshape(1, -1), p, state)


HG_HB = 4


def _hgrn_kernel(q_ref, f_ref, i_ref, g_ref, lb_ref, nw_ref, p_ref, s0_ref,
                 o_ref, s_ref, *, c):
    ci = pl.program_id(2)

    @pl.when(ci == 0)
    def _():
        s_ref[...] = s0_ref[...]

    lb = lb_ref[...]
    log_lb = jnp.log(jnp.maximum(lb, TINY))
    zf = f_ref[...]
    a = log_lb
    cc = jnp.log1p(-lb) + _log_sigmoid(zf)
    log_f = jnp.maximum(a, cc) + jnp.log1p(jnp.exp(-jnp.abs(a - cc)))
    key = (1.0 - lb) * jax.nn.sigmoid(-zf)
    q = _silu(q_ref[...])
    v = i_ref[...]
    gate = g_ref[...]
    s_cat = jnp.concatenate([s_ref[0, hh] for hh in range(HG_HB)], axis=-1)
    o, s_new = _linear_attn_block(q, key, v, log_f, s_cat, p_ref, c, HG_HB, HG_DV)
    for hh in range(HG_HB):
        sl = slice(hh * HG_DV, (hh + 1) * HG_DV)
        s_ref[0, hh] = s_new[:, sl]
        o_ref[:, sl] = _gated_head_out(o[hh * c:(hh + 1) * c], gate[:, sl],
                                       nw_ref[...]).astype(o_ref.dtype)


def hgrn_mixer(proj, lb, norm_w, state, b, t, c):
    col0 = 0
    m = proj.shape[0]
    nc = t // c
    ng = HG_H // HG_HB
    w = HG_HB * HG_DK
    p = jnp.asarray(_prefix_matrix(c), BF16)
    row = lambda bb, g, cc: bb * nc + cc
    kern = functools.partial(_hgrn_kernel, c=c)
    return pl.pallas_call(
        kern,
        grid=(b, ng, nc),
        in_specs=[pl.BlockSpec((c, w), lambda bb, g, cc: (row(bb, g, cc), col0 + g)),
                  pl.BlockSpec((c, w), lambda bb, g, cc: (row(bb, g, cc), col0 + ng + g)),
                  pl.BlockSpec((c, w), lambda bb, g, cc: (row(bb, g, cc), col0 + 2 * ng + g)),
                  pl.BlockSpec((c, w), lambda bb, g, cc: (row(bb, g, cc), col0 + 3 * ng + g)),
                  pl.BlockSpec((1, w), lambda bb, g, cc: (0, g)),
                  pl.BlockSpec((1, HG_DV), lambda bb, g, cc: (0, 0)),
                  pl.BlockSpec(p.shape, lambda bb, g, cc: (0, 0)),
                  pl.BlockSpec((1, HG_HB, HG_DK, HG_DV), lambda bb, g, cc: (bb, g, 0, 0))],
        out_specs=[pl.BlockSpec((c, w), lambda bb, g, cc: (row(bb, g, cc), g)),
                   pl.BlockSpec((1, HG_HB, HG_DK, HG_DV), lambda bb, g, cc: (bb, g, 0, 0))],
        out_shape=[jax.ShapeDtypeStruct((m, HG_H * HG_DV), BF16),
                   jax.ShapeDtypeStruct((b, HG_H, HG_DK, HG_DV), F32)],
        compiler_params=_cparams(("parallel", "parallel", "arbitrary")),
        name="hgrn_mixer",
    )(proj, proj, proj, proj, lb.reshape(1, -1), norm_w.reshape(1, -1), p, state)


GDN_HB = 4
GDN_GS = 4


def _each(f, *lists):
    return [f(*args) for args in zip(*lists)]


def _unit_lower_inverse(ns, r, c):
    ri = lax.broadcasted_iota(jnp.int32, (r, r), 0)
    ci = lax.broadcasted_iota(jnp.int32, (r, r), 1)
    eye = (ri == ci).astype(F32)
    d = [jnp.where((ri >> 4) == (ci >> 4), n, 0.0) for n in ns]
    x = [eye - d_ for d_ in d]
    p = d
    for _ in range(3):
        p = _each(_mm, p, p)
        x = _each(lambda x_, p_: x_ + _mm(x_, p_), x, p)
    size = 16
    while size < c:
        sh = size.bit_length() - 1
        mask = ((ri >> (sh + 1)) == (ci >> (sh + 1))) & ((ri >> sh) > (ci >> sh))
        lx = _each(lambda n, x_: _mm(jnp.where(mask, n, 0.0), x_), ns, x)
        x = _each(lambda x_, lx_: x_ - _mm(x_, lx_), x, lx)
        size *= 2
    return x


def _gdn_blocks(q, k, v, beta, b_c, b_last, s_cat, c, nh):
    r = nh * c
    ri = lax.broadcasted_iota(jnp.int32, (r, r), 0)
    ci = lax.broadcasted_iota(jnp.int32, (r, r), 1)
    shc = c.bit_length() - 1
    causal = ((ri >> shc) == (ci >> shc)) & (ri >= ci)

    def decay_of(bc):
        b_r = jnp.sum(jnp.where(ri == ci, bc, 0.0), axis=0, keepdims=True)
        return jnp.where(causal, jnp.exp(jnp.minimum(bc - b_r, 0.0)), 0.0)

    decay = _each(decay_of, b_c)
    kb = [k_.astype(BF16) for k_ in k]
    kq = _each(lambda kb_, q_: _dot_nt(jnp.concatenate([kb_, q_.astype(BF16)], axis=0), kb_), kb, q)
    n = _each(lambda be, kq_, de: jnp.where(ri > ci, be * kq_[:r] * de, 0.0), beta, kq, decay)
    tinv = _unit_lower_inverse(n, r, c)
    eb = [jnp.exp(bc) for bc in b_c]
    sol = _each(lambda t_, v_, k_, be, e_: _mm(t_, jnp.concatenate([v_ * be, k_ * (be * e_)], axis=-1)),
                tinv, v, k, beta, eb)
    ws_qs = _each(lambda so, q_, e_, s_: _mm(jnp.concatenate([so[:, GDN_DV:], q_ * e_], axis=0), s_),
                  sol, q, eb, s_cat)
    v_new = _each(lambda so, wq: so[:, :GDN_DV] - _diag_blocks(wq[:r], nh, c, GDN_DV), sol, ws_qs)
    o = _each(lambda wq, kq_, de, vn: _diag_blocks(wq[r:], nh, c, GDN_DV) + _mm(kq_[r:] * de, vn),
              ws_qs, kq, decay, v_new)
    upd = _each(lambda k_, bl, bc, vn: _dot_tn((k_ * jnp.exp(bl - bc)).astype(BF16),
                                               _block_cols(vn, nh, c).astype(BF16)),
                k, b_last, b_c, v_new)

    def decayed(bl, s_, up):
        el = jnp.exp(bl)
        dec = jnp.concatenate(
            [jnp.broadcast_to(el[i * c:i * c + 1], (1, GDN_DV)) for i in range(nh)], axis=-1)
        return dec * s_ + up

    return o, _each(decayed, b_last, s_cat, upd)


def _gdn_kernel(q_ref, k_ref, v_ref, z_ref, gc_ref, wq_ref, wk_ref, wv_ref,
                pc_ref, nw_ref, tril_ref, s0_ref, cq0_ref, ck0_ref, cv0_ref,
                o_ref, s_ref, cq_ref, ck_ref, cv_ref,
                xq, xk, xv, *, c, nc):
    ci = pl.program_id(2)
    last = GDN_CONV - 1

    @pl.when(ci == 0)
    def _():
        s_ref[...] = s0_ref[...]

    def conv(x_ref, w_ref, c0_ref, c_ref, scr):
        @pl.when(ci == 0)
        def _():
            scr[8 - last:8, :] = c0_ref[0]

        scr[8:8 + c, :] = x_ref[...]
        y = scr[8:8 + c, :] * w_ref[last:last + 1, :]
        for j in range(last):
            y = y + scr[8 - last + j:8 - last + j + c, :] * w_ref[j:j + 1, :]
        tail = scr[8 + c - last:8 + c, :]
        scr[8 - last:8, :] = tail

        @pl.when(ci == nc - 1)
        def _():
            c_ref[0] = tail

        return _silu(y)

    qa = conv(q_ref, wq_ref, cq0_ref, cq_ref, xq)
    ka = conv(k_ref, wk_ref, ck0_ref, ck_ref, xk)
    va = conv(v_ref, wv_ref, cv0_ref, cv_ref, xv)

    nh = GDN_HB
    gw = nh * GDN_DK
    q, k, v, beta, b_c, b_last, s_cat = [], [], [], [], [], [], []
    for gi in range(GDN_GS):
        gcol = gc_ref[gi]
        pc = pc_ref[gi]
        beta_all = jax.nn.sigmoid(gcol)
        g_c = pc[0:1, :] * _softplus(gcol + pc[1:2, :])
        b_col = _sel_dot(tril_ref[...], g_c)
        beta.append(jnp.concatenate([beta_all[:, hh:hh + 1] for hh in range(nh)], axis=0))
        b_c.append(jnp.concatenate([b_col[:, nh + hh:nh + hh + 1] for hh in range(nh)], axis=0))
        b_last.append(jnp.concatenate(
            [jnp.broadcast_to(b_col[c - 1:c, nh + hh:nh + hh + 1], (c, 1)) for hh in range(nh)],
            axis=0))
        gs = slice(gi * gw, (gi + 1) * gw)
        qg = _stack_heads(qa[:, gs], nh, GDN_DK)
        kg = _stack_heads(ka[:, gs], nh, GDN_DK)
        q.append(qg * lax.rsqrt(jnp.sum(qg * qg, axis=-1, keepdims=True) + EPS) * (GDN_DK ** -0.5))
        k.append(kg * lax.rsqrt(jnp.sum(kg * kg, axis=-1, keepdims=True) + EPS))
        v.append(_stack_heads(va[:, gs], nh, GDN_DV))
        s_cat.append(jnp.concatenate([s_ref[0, gi * nh + hh] for hh in range(nh)], axis=-1))
    o, s_new = _gdn_blocks(q, k, v, beta, b_c, b_last, s_cat, c, nh)
    z = z_ref[...]
    for gi in range(GDN_GS):
        for hh in range(nh):
            sl = slice(hh * GDN_DV, (hh + 1) * GDN_DV)
            zl = slice(gi * gw + hh * GDN_DV, gi * gw + (hh + 1) * GDN_DV)
            s_ref[0, gi * nh + hh] = s_new[gi][:, sl]
            o_ref[:, zl] = _gated_head_out(o[gi][hh * c:(hh + 1) * c], z[:, zl],
                                           nw_ref[...]).astype(o_ref.dtype)


def gdn_mixer(proj, gates_c, conv_w, par_c, norm_w, state, cache, b, t, c):
    m = proj.shape[0]
    nc = t // c
    hs = GDN_HB * GDN_GS
    ng = GDN_H // hs
    w = hs * GDN_DK
    col0 = 0
    last = GDN_CONV - 1
    tril = jnp.asarray(np.tril(np.ones((c, c), np.float32)), BF16)
    row = lambda bb, g, cc: bb * nc + cc
    kern = functools.partial(_gdn_kernel, c=c, nc=nc)
    big = lambda off: pl.BlockSpec((c, w), lambda bb, g, cc: (row(bb, g, cc), col0 + off * ng + g))
    cw = lambda off: pl.BlockSpec((GDN_CONV, w), lambda bb, g, cc: (0, off * ng + g))
    c0 = lambda off: pl.BlockSpec((1, last, w), lambda bb, g, cc: (bb, 0, off * ng + g))
    cout = pl.BlockSpec((1, last, w), lambda bb, g, cc: (bb, 0, g))
    outs = pl.pallas_call(
        kern,
        grid=(b, ng, nc),
        in_specs=[big(0), big(1), big(2), big(3),
                  pl.BlockSpec((GDN_GS, c, LANE), lambda bb, g, cc: (g, row(bb, g, cc), 0)),
                  cw(0), cw(1), cw(2),
                  pl.BlockSpec((GDN_GS, 2, LANE), lambda bb, g, cc: (g, 0, 0)),
                  pl.BlockSpec((1, GDN_DV), lambda bb, g, cc: (0, 0)),
                  pl.BlockSpec((c, c), lambda bb, g, cc: (0, 0)),
                  pl.BlockSpec((1, hs, GDN_DK, GDN_DV), lambda bb, g, cc: (bb, g, 0, 0)),
                  c0(0), c0(1), c0(2)],
        out_specs=[pl.BlockSpec((c, w), lambda bb, g, cc: (row(bb, g, cc), g)),
                   pl.BlockSpec((1, hs, GDN_DK, GDN_DV), lambda bb, g, cc: (bb, g, 0, 0)),
                   cout, cout, cout],
        out_shape=[jax.ShapeDtypeStruct((m, GDN_H * GDN_DV), BF16),
                   jax.ShapeDtypeStruct((b, GDN_H, GDN_DK, GDN_DV), F32),
                   jax.ShapeDtypeStruct((b, last, GDN_H * GDN_DK), F32),
                   jax.ShapeDtypeStruct((b, last, GDN_H * GDN_DK), F32),
                   jax.ShapeDtypeStruct((b, last, GDN_H * GDN_DV), F32)],
        scratch_shapes=[pltpu.VMEM((8 + c, w), F32), pltpu.VMEM((8 + c, w), F32),
                        pltpu.VMEM((8 + c, w), F32)],
        compiler_params=_cparams(("parallel", "parallel", "arbitrary")),
        name="gdn_mixer",
    )(proj, proj, proj, proj, gates_c, conv_w, conv_w, conv_w, par_c,
      norm_w.reshape(1, -1), tril, state, cache, cache, cache)
    o, s, cq, ck, cv = outs
    return o, s, jnp.concatenate([cq, ck, cv], axis=-1)


def _row_tile(m, cap):
    tm = min(m, cap)
    while m % tm:
        tm //= 2
    return tm


def _group_gates(small):
    m = small.shape[0]
    ng = GDN_H // GDN_HB
    db = small[:, GLA_RANK:GLA_RANK + GDN_H].reshape(m, ng, GDN_HB)
    da = small[:, GLA_RANK + GDN_H:GLA_RANK + 2 * GDN_H].reshape(m, ng, GDN_HB)
    g = jnp.concatenate([db, da, jnp.zeros((m, ng, LANE - 2 * GDN_HB), F32)], axis=-1)
    return jnp.transpose(g, (1, 0, 2))


def _group_params(a_log, dt_bias):
    ng = GDN_H // GDN_HB
    na = -jnp.exp(a_log.astype(F32)).reshape(ng, GDN_HB)
    dt = dt_bias.astype(F32).reshape(ng, GDN_HB)
    pad = jnp.zeros((ng, GDN_HB), F32)
    tail = jnp.zeros((ng, LANE - 2 * GDN_HB), F32)
    return jnp.stack([jnp.concatenate([pad, na, tail], axis=-1),
                      jnp.concatenate([pad, dt, tail], axis=-1)], axis=1)


COL_GLA = 0
N_GLA = 2 * GLA_H * GLA_DK + 2 * GLA_H * GLA_DV
COL_LR = COL_GLA + N_GLA
COL_GDN = COL_LR + GLA_RANK
N_GDN = 2 * GDN_H * GDN_DK + 2 * GDN_H * GDN_DV
COL_BA = COL_GDN + N_GDN
COL_HG = COL_BA + 2 * GDN_H
N_HG = 2 * HG_H * HG_DK + 2 * HG_H * HG_DV

COL_TILE = 256
DOWN_TILES = (1024, 512)


def _prep_weights(w_gla_gate, w_down):
    wg_pad = jnp.pad(w_gla_gate.astype(BF16), ((0, 0), (0, LANE - GLA_RANK), (0, 0)))
    return wg_pad, w_down.astype(BF16)


def _trunk(x, pe, st_gla, st_gdn, cb_gdn, st_hg, cb_ffn, wts, prm, lower_bounds, norm_final):
    (w_in, wg_pad, w_out, w_up, w_down, w_pg, w_pp) = wts
    (norm_mix, b_gla_gate, gla_norm, w_gdn_conv, gdn_a_log, gdn_dt_bias, gdn_norm, hgrn_norm,
     norm_ffn, w_ffn_conv, norm_ple) = prm
    b, t, d = x.shape
    depth = w_in.shape[0]
    m = b * t
    c = min(MAX_CHUNK, t)
    tm = _row_tile(m, ROW_TILE)
    tr = _row_tile(t, 256)
    tn = COL_TILE
    f = w_down.shape[1]
    h = x.reshape(m, d)
    n_gla, n_gdn, n_gconv, n_hg, n_fconv = [], [], [], [], []
    for li in range(depth):
        xn = rmsnorm(h, norm_mix[li], BF16, _row_tile(m, 256))
        p_gla = matmul(xn, w_in, li, F32, tm, tn, COL_GLA, N_GLA)
        p_gdn = matmul(xn, w_in, li, F32, tm, tn, COL_GDN, N_GDN)
        p_hg = matmul(xn, w_in, li, F32, tm, tn, COL_HG, N_HG)
        small = gate_proj(xn, w_in, li, _row_tile(m, 1024), COL_LR, GLA_RANK, COL_BA, 2 * GDN_H)

        o_gla, s_gla = gla_mixer(p_gla, small, wg_pad[li], b_gla_gate[li], gla_norm[li],
                                 st_gla[li], b, t, c)
        o_gdn, s_gdn, b_gdn = gdn_mixer(p_gdn, _group_gates(small), w_gdn_conv[li],
                                        _group_params(gdn_a_log[li], gdn_dt_bias[li]), gdn_norm[li],
                                        st_gdn[li], cb_gdn[li], b, t, c)
        o_hg, s_hg = hgrn_mixer(p_hg, lower_bounds[li], hgrn_norm[li], st_hg[li], b, t, c)
        mix = jnp.concatenate([o_gla, o_gdn, o_hg], axis=-1)
        h = matmul_residual(mix, w_out, li, h, tm, tn, mix.shape[1])

        xn = rmsnorm(h, norm_ffn[li], BF16, _row_tile(m, 256))
        if t == tm:
            hid, cg, cv = ffn_up_fused(xn, w_up, li, w_ffn_conv[li], cb_ffn[li], tn)
        else:
            up = matmul(xn, w_up, li, F32, tm, tn)
            hid, cg, cv = ffn_conv_act(up, w_ffn_conv[li], cb_ffn[li], b, t, tr, f // 2)
        b_ffn = jnp.concatenate([cg, cv], axis=-1)
        h = matmul_residual(hid, w_down, li, h, _row_tile(m, DOWN_TILES[0]), DOWN_TILES[1], f // 2)

        xn = rmsnorm(h, norm_ple[li], BF16, _row_tile(m, 256))
        h = ple_update(xn, w_pg, pe[li].reshape(m, -1).astype(BF16), w_pp, li, h, tm, tn)

        n_gla.append(s_gla)
        n_gdn.append(s_gdn)
        n_gconv.append(b_gdn)
        n_hg.append(s_hg)
        n_fconv.append(b_ffn)
    y = rmsnorm(h, norm_final, F32, _row_tile(m, 256)).reshape(b, t, d)
    return (y, jnp.stack(n_gla), jnp.stack(n_gdn), jnp.stack(n_gconv),
            jnp.stack(n_hg), jnp.stack(n_fconv))


def kernel(x_prompt, x_sample, p_prompt, p_sample, state_gla, state_gdn, cache_gdn_conv, state_hgrn, cache_ffn_conv, norm_mix, w_in, w_gla_gate, b_gla_gate, gla_norm, w_gdn_conv, gdn_a_log, gdn_dt_bias, gdn_norm, hgrn_lb, hgrn_norm, w_out, norm_ffn, w_up, w_ffn_conv, w_down, norm_ple, w_ple_gate, w_ple_proj, norm_final):
    depth = w_in.shape[0]
    bp = x_prompt.shape[0]
    wg_pad, w_down_b = _prep_weights(w_gla_gate, w_down)
    wts = (w_in, wg_pad, w_out, w_up, w_down_b, w_ple_gate, w_ple_proj)
    prm = (norm_mix, b_gla_gate, gla_norm, w_gdn_conv, gdn_a_log, gdn_dt_bias, gdn_norm,
           hgrn_norm, norm_ffn, w_ffn_conv, norm_ple)
    sm = jax.nn.softmax(hgrn_lb.astype(F32), axis=0)
    lower_bounds = jnp.cumsum(sm, axis=0) - sm[0]
    zeros = lambda a: jnp.zeros((depth, bp) + a.shape[2:], F32)
    prompt = _trunk(x_prompt, p_prompt, zeros(state_gla), zeros(state_gdn), zeros(cache_gdn_conv),
                    zeros(state_hgrn), zeros(cache_ffn_conv), wts, prm, lower_bounds, norm_final)
    sample = _trunk(x_sample, p_sample, state_gla, state_gdn, cache_gdn_conv, state_hgrn,
                    cache_ffn_conv, wts, prm, lower_bounds, norm_final)
    return (prompt[0], sample[0]) + prompt[1:] + sample[1:]
```

```python
import functools

import numpy as np
import jax
import jax.numpy as jnp
from jax import lax
from jax.experimental import pallas as pl
from jax.experimental.pallas import tpu as pltpu

F32 = jnp.float32
BF16 = jnp.bfloat16

EPS = 1e-6
TINY = 1e-30
GLA_GATE_NORM = 16.0
GLA_H, GLA_DK, GLA_DV, GLA_RANK = 4, 128, 256, 16
GDN_H, GDN_DK, GDN_DV, GDN_CONV = 16, 128, 128, 4
HG_H, HG_DK, HG_DV = 8, 128, 128
FFN_CONV = 3
MAX_CHUNK = 64
ROW_TILE = 2048

LANE = 128
VMEM_LIMIT = 56 * 1024 * 1024


def _cparams(sem):
    return pltpu.CompilerParams(dimension_semantics=sem, vmem_limit_bytes=VMEM_LIMIT)


def _dot(a, b):
    return jnp.dot(a, b, preferred_element_type=F32)


def _dot_nt(a, b):
    return lax.dot_general(a, b, (((1,), (1,)), ((), ())), preferred_element_type=F32)


def _dot_tn(a, b):
    return lax.dot_general(a, b, (((0,), (0,)), ((), ())), preferred_element_type=F32)


def _mm(a, b):
    return _dot(a.astype(BF16), b.astype(BF16))


def _split3(x):
    hi = x.astype(BF16)
    r = x - hi.astype(F32)
    mid = r.astype(BF16)
    lo = (r - mid.astype(F32)).astype(BF16)
    return hi, mid, lo


def _sel_dot(p, x):
    hi, mid, lo = _split3(x)
    return _dot(p, hi) + _dot(p, mid) + _dot(p, lo)


def _dot_sel(x, p):
    hi, mid, lo = _split3(x)
    return _dot(hi, p) + _dot(mid, p) + _dot(lo, p)


def _softplus(x):
    return jnp.maximum(x, 0.0) + jnp.log1p(jnp.exp(-jnp.abs(x)))


def _log_sigmoid(x):
    return -_softplus(-x)


def _silu(x):
    return x * jax.nn.sigmoid(x)


def _rmsnorm_kernel(x_ref, w_ref, o_ref):
    x = x_ref[...]
    y = x * lax.rsqrt(jnp.mean(x * x, axis=-1, keepdims=True) + EPS)
    o_ref[...] = (y * w_ref[...]).astype(o_ref.dtype)


def rmsnorm(x, w, out_dtype, tm):
    m, d = x.shape
    return pl.pallas_call(
        _rmsnorm_kernel,
        grid=(m // tm,),
        in_specs=[pl.BlockSpec((tm, d), lambda i: (i, 0)),
                  pl.BlockSpec((1, d), lambda i: (0, 0))],
        out_specs=pl.BlockSpec((tm, d), lambda i: (i, 0)),
        out_shape=jax.ShapeDtypeStruct((m, d), out_dtype),
        compiler_params=_cparams(("parallel",)),
        name="rmsnorm",
    )(x, w.reshape(1, d))


SINGLE_BUFFER_BYTES = 16 * 1024 * 1024


def _act_spec(shape, index_map, itemsize=2):
    if shape[0] * shape[1] * itemsize >= SINGLE_BUFFER_BYTES:
        return pl.BlockSpec(shape, index_map, pipeline_mode=pl.Buffered(1))
    return pl.BlockSpec(shape, index_map)


def _mm_kernel(x_ref, w_ref, o_ref):
    o_ref[...] = _dot(x_ref[...], w_ref[...].astype(BF16)).astype(o_ref.dtype)


def matmul(x, w, li, out_dtype, tm, tn):
    m, k = x.shape
    n = w.shape[2]
    return pl.pallas_call(
        _mm_kernel,
        grid=(m // tm, n // tn),
        in_specs=[_act_spec((tm, k), lambda i, j: (i, 0)),
                  pl.BlockSpec((None, k, tn), lambda i, j: (li, 0, j))],
        out_specs=pl.BlockSpec((tm, tn), lambda i, j: (i, j)),
        out_shape=jax.ShapeDtypeStruct((m, n), out_dtype),
        compiler_params=_cparams(("parallel", "arbitrary")),
        name="matmul",
    )(x, w)


def _window_spec(rows, k, index_map):
    return pl.BlockSpec((pl.Element(1), pl.Element(rows), pl.Element(k)), index_map)


def _mm_t_kernel(x_ref, wt_ref, o_ref):
    o_ref[...] = _dot_nt(x_ref[...], wt_ref[0].astype(BF16)).astype(o_ref.dtype)


def matmul_t(x, wt, li, out_dtype, tm, tn, row0, nrows):
    m, k = x.shape
    assert row0 % 8 == 0 and nrows % tn == 0
    return pl.pallas_call(
        _mm_t_kernel,
        grid=(m // tm, nrows // tn),
        in_specs=[_act_spec((tm, k), lambda i, j: (i, 0)),
                  _window_spec(tn, k, lambda i, j: (li, (row0 // 8 + j * (tn // 8)) * 8, 0))],
        out_specs=pl.BlockSpec((tm, tn), lambda i, j: (i, j)),
        out_shape=jax.ShapeDtypeStruct((m, nrows), out_dtype),
        compiler_params=_cparams(("parallel", "arbitrary")),
        name="matmul_t",
    )(x, wt)


def _gate_proj_kernel(x_ref, wa_ref, wb_ref, o_ref):
    x = x_ref[...]
    a = _dot_nt(x, wa_ref[0].astype(BF16))
    b = _dot_nt(x, wb_ref[0].astype(BF16))
    pad = jnp.zeros((x.shape[0], LANE - a.shape[1] - b.shape[1]), F32)
    o_ref[...] = jnp.concatenate([a, b, pad], axis=-1)


def gate_proj(x, wt, li, tm, row_a, n_a, row_b, n_b):
    m, k = x.shape
    assert row_a % 8 == 0 and row_b % 8 == 0
    return pl.pallas_call(
        _gate_proj_kernel,
        grid=(m // tm,),
        in_specs=[pl.BlockSpec((tm, k), lambda i: (i, 0)),
                  _window_spec(n_a, k, lambda i: (li, row_a, 0)),
                  _window_spec(n_b, k, lambda i: (li, row_b, 0))],
        out_specs=pl.BlockSpec((tm, LANE), lambda i: (i, 0)),
        out_shape=jax.ShapeDtypeStruct((m, LANE), F32),
        compiler_params=_cparams(("parallel",)),
        name="gate_proj",
    )(x, wt, wt)


def _mm_res_kernel(x_ref, w_ref, r_ref, o_ref):
    k = pl.program_id(2)

    @pl.when(k == 0)
    def _():
        o_ref[...] = r_ref[...]

    o_ref[...] += _dot(x_ref[...], w_ref[...].astype(BF16))


def matmul_residual(x, w, li, res, tm, tn, tk):
    m, k = x.shape
    n = w.shape[2]
    return pl.pallas_call(
        _mm_res_kernel,
        grid=(m // tm, n // tn, k // tk),
        in_specs=[_act_spec((tm, tk), lambda i, j, kk: (i, kk)) if k == tk
                  else pl.BlockSpec((tm, tk), lambda i, j, kk: (i, kk)),
                  pl.BlockSpec((None, tk, tn), lambda i, j, kk: (li, kk, j)),
                  pl.BlockSpec((tm, tn), lambda i, j, kk: (i, j))],
        out_specs=pl.BlockSpec((tm, tn), lambda i, j, kk: (i, j)),
        out_shape=jax.ShapeDtypeStruct((m, n), F32),
        compiler_params=_cparams(("parallel", "arbitrary", "arbitrary")),
        name="matmul_residual",
    )(x, w, res)


def _ple_kernel(x_ref, wg_ref, p_ref, wp_ref, r_ref, o_ref):
    gate = jax.nn.sigmoid(_dot(x_ref[...], wg_ref[...].astype(BF16)))
    o_ref[...] = r_ref[...] + gate * _dot(p_ref[...], wp_ref[...].astype(BF16))


def ple_update(xn, wg, pe, wp, li, res, tm, tn):
    m, k = xn.shape
    n = wg.shape[2]
    kp = pe.shape[1]
    return pl.pallas_call(
        _ple_kernel,
        grid=(m // tm, n // tn),
        in_specs=[_act_spec((tm, k), lambda i, j: (i, 0)),
                  pl.BlockSpec((None, k, tn), lambda i, j: (li, 0, j)),
                  pl.BlockSpec((tm, kp), lambda i, j: (i, 0)),
                  pl.BlockSpec((None, kp, tn), lambda i, j: (li, 0, j)),
                  pl.BlockSpec((tm, tn), lambda i, j: (i, j))],
        out_specs=pl.BlockSpec((tm, tn), lambda i, j: (i, j)),
        out_shape=jax.ShapeDtypeStruct((m, n), F32),
        compiler_params=_cparams(("parallel", "arbitrary")),
        name="ple_update",
    )(xn, wg, pe, wp, res)


def _ffn_act_kernel(g_ref, v_ref, wg_ref, wv_ref, cg0_ref, cv0_ref,
                    h_ref, cg_ref, cv_ref, sg, sv, *, tr, n_row_tiles):
    i = pl.program_id(2)
    last = FFN_CONV - 1

    def conv(x_ref, w_ref, c0_ref, c_ref, scr):
        @pl.when(i == 0)
        def _():
            scr[8 - last:8, :] = c0_ref[0]

        scr[8:8 + tr, :] = x_ref[...]
        y = scr[8:8 + tr, :] * w_ref[last:last + 1, :]
        for j in range(last):
            y = y + scr[8 - last + j:8 - last + j + tr, :] * w_ref[j:j + 1, :]
        tail = scr[8 + tr - last:8 + tr, :]
        scr[8 - last:8, :] = tail

        @pl.when(i == n_row_tiles - 1)
        def _():
            c_ref[0] = tail

        return y

    gate = conv(g_ref, wg_ref, cg0_ref, cg_ref, sg)
    val = conv(v_ref, wv_ref, cv0_ref, cv_ref, sv)
    h_ref[...] = (_silu(gate) * val).astype(h_ref.dtype)


def ffn_conv_act(up, w_conv, cache, b, t, tr, tc):
    m, f2 = up.shape
    f = f2 // 2
    nj = f // tc
    n_row_tiles = t // tr
    last = FFN_CONV - 1
    kern = functools.partial(_ffn_act_kernel, tr=tr, n_row_tiles=n_row_tiles)
    return pl.pallas_call(
        kern,
        grid=(b, nj, n_row_tiles),
        in_specs=[pl.BlockSpec((tr, tc), lambda bb, j, i: (bb * n_row_tiles + i, j)),
                  pl.BlockSpec((tr, tc), lambda bb, j, i: (bb * n_row_tiles + i, nj + j)),
                  pl.BlockSpec((FFN_CONV, tc), lambda bb, j, i: (0, j)),
                  pl.BlockSpec((FFN_CONV, tc), lambda bb, j, i: (0, nj + j)),
                  pl.BlockSpec((1, last, tc), lambda bb, j, i: (bb, 0, j)),
                  pl.BlockSpec((1, last, tc), lambda bb, j, i: (bb, 0, nj + j))],
        out_specs=[pl.BlockSpec((tr, tc), lambda bb, j, i: (bb * n_row_tiles + i, j)),
                   pl.BlockSpec((1, last, tc), lambda bb, j, i: (bb, 0, j)),
                   pl.BlockSpec((1, last, tc), lambda bb, j, i: (bb, 0, j))],
        out_shape=[jax.ShapeDtypeStruct((m, f), BF16),
                   jax.ShapeDtypeStruct((b, last, f), F32),
                   jax.ShapeDtypeStruct((b, last, f), F32)],
        scratch_shapes=[pltpu.VMEM((8 + tr, tc), F32), pltpu.VMEM((8 + tr, tc), F32)],
        compiler_params=_cparams(("parallel", "parallel", "arbitrary")),
        name="ffn_conv_act",
    )(up, up, w_conv, w_conv, cache, cache)


def _ffn_up_kernel(x_ref, wg_ref, wv_ref, cwg_ref, cwv_ref, cg0_ref, cv0_ref,
                   h_ref, cg_ref, cv_ref, *, tm):
    last = FFN_CONV - 1
    x = x_ref[...]
    row = lax.broadcasted_iota(jnp.int32, (8, 1), 0)

    def conv(w_ref, cw_ref, c0_ref, c_ref):
        up = _dot(x, w_ref[...].astype(BF16))
        c_ref[0] = up[tm - last:tm, :]
        c0 = c0_ref[0]
        y = up * cw_ref[last:last + 1, :]
        for s in range(1, last + 1):
            sh = pltpu.roll(up, s, axis=0)
            head = sh[0:8]
            for r in range(s):
                head = jnp.where(row == r, c0[last - s + r:last - s + r + 1, :], head)
            y = y + jnp.concatenate([head, sh[8:]], axis=0) * cw_ref[last - s:last - s + 1, :]
        return y

    gate = conv(wg_ref, cwg_ref, cg0_ref, cg_ref)
    val = conv(wv_ref, cwv_ref, cv0_ref, cv_ref)
    h_ref[...] = (_silu(gate) * val).astype(h_ref.dtype)


def ffn_up_fused(xn, w_up, li, w_conv, cache, tn):
    m, d = xn.shape
    b = cache.shape[0]
    tm = m // b
    f = w_up.shape[2] // 2
    nj = f // tn
    last = FFN_CONV - 1
    kern = functools.partial(_ffn_up_kernel, tm=tm)
    cspec = lambda off: pl.BlockSpec((1, last, tn), lambda i, j: (i, 0, off + j))
    return pl.pallas_call(
        kern,
        grid=(b, nj),
        in_specs=[_act_spec((tm, d), lambda i, j: (i, 0)),
                  pl.BlockSpec((None, d, tn), lambda i, j: (li, 0, j)),
                  pl.BlockSpec((None, d, tn), lambda i, j: (li, 0, nj + j)),
                  pl.BlockSpec((FFN_CONV, tn), lambda i, j: (0, j)),
                  pl.BlockSpec((FFN_CONV, tn), lambda i, j: (0, nj + j)),
                  cspec(0), cspec(nj)],
        out_specs=[pl.BlockSpec((tm, tn), lambda i, j: (i, j)), cspec(0), cspec(0)],
        out_shape=[jax.ShapeDtypeStruct((m, f), BF16),
                   jax.ShapeDtypeStruct((b, last, f), F32),
                   jax.ShapeDtypeStruct((b, last, f), F32)],
        compiler_params=_cparams(("parallel", "arbitrary")),
        name="ffn_up_fused",
    )(xn, w_up, w_up, w_conv, w_conv, cache, cache)


def _levels(c):
    out, h = [], c // 2
    while h >= 1:
        out.append(h)
        h //= 2
    return out


def _prefix_matrix(c):
    blocks = [np.tril(np.ones((c, c), np.float32))]
    for h in _levels(c):
        p = np.zeros((c, c), np.float32)
        for t in range(c):
            mid = (t // (2 * h)) * 2 * h + h
            if t % (2 * h) >= h:
                p[t, mid:t + 1] = 1.0
            else:
                p[t, t + 1:mid] = 1.0
        blocks.append(p)
    return np.concatenate(blocks, axis=0)


def _gated_head_out(o, gate, w):
    o = o * lax.rsqrt(jnp.mean(o * o, axis=-1, keepdims=True) + EPS) * w
    return o * _silu(gate)


def _stack_heads(x, nh, w):
    return jnp.concatenate([x[:, i * w:(i + 1) * w] for i in range(nh)], axis=0)


def _diag_blocks(y, nh, c, w):
    return jnp.concatenate([y[i * c:(i + 1) * c, i * w:(i + 1) * w] for i in range(nh)], axis=0)


def _block_cols(v, nh, c):
    head = lax.broadcasted_iota(jnp.int32, (nh * c, 1), 0) >> (c.bit_length() - 1)
    return jnp.concatenate([jnp.where(head == i, v, 0.0) for i in range(nh)], axis=-1)


def _row_to_col(row):
    n = row.shape[1]
    eye = (lax.broadcasted_iota(jnp.int32, (n, n), 0) == lax.broadcasted_iota(jnp.int32, (n, n), 1))
    return jnp.sum(jnp.where(eye, row, 0.0), axis=1, keepdims=True)


def _linear_attn_block(q, k, v, g, s_cat, p_ref, c, nh, dv):
    dk = q.shape[1] // nh
    r = nh * c
    z = _sel_dot(p_ref[...], g)
    b = z[0:c]
    row = lax.broadcasted_iota(jnp.int32, (c, 1), 0)
    ri = lax.broadcasted_iota(jnp.int32, (r, r), 0)
    ci = lax.broadcasted_iota(jnp.int32, (r, r), 1)
    qk = _stack_heads(q * k, nh, dk)
    attn = jnp.where(ri == ci, jnp.sum(qk, axis=-1, keepdims=True), 0.0)
    for li, h in enumerate(_levels(c)):
        e = jnp.exp(z[(li + 1) * c:(li + 2) * c])
        lower = (row & (2 * h - 1)) >= h
        ql = _stack_heads(jnp.where(lower, q * e, 0.0), nh, dk).astype(BF16)
        kl = _stack_heads(jnp.where(lower, 0.0, k * e), nh, dk).astype(BF16)
        sh = (2 * h).bit_length() - 1
        attn = attn + jnp.where((ri >> sh) == (ci >> sh), _dot_nt(ql, kl), 0.0)
    vs = _stack_heads(v, nh, dv)
    qe = _stack_heads(q * jnp.exp(b), nh, dk)
    o = _diag_blocks(_mm(qe, s_cat), nh, c, dv) + _mm(attn, vs)
    b_last = b[c - 1:c]
    kt = _stack_heads(k * jnp.exp(b_last - b), nh, dk).astype(BF16)
    upd = _dot_tn(kt, _block_cols(vs, nh, c).astype(BF16))
    el = jnp.exp(b_last)
    dec = jnp.concatenate(
        [jnp.broadcast_to(_row_to_col(el[:, i * dk:(i + 1) * dk]), (dk, dv)) for i in range(nh)], axis=-1)
    return o, dec * s_cat + upd


def _gla_kernel(q_ref, k_ref, v_ref, gg_ref, sm_ref, wg_ref, bg_ref, nw_ref, p_ref, s0_ref,
                o_ref, s_ref, *, c):
    ci = pl.program_id(1)

    @pl.when(ci == 0)
    def _():
        s_ref[...] = s0_ref[...]

    x = _dot_sel(sm_ref[...], wg_ref[...]) + bg_ref[...]
    g = _log_sigmoid(x) * (1.0 / GLA_GATE_NORM)
    q = q_ref[...] * (GLA_DK ** -0.5)
    gate = gg_ref[...]
    s_cat = jnp.concatenate([s_ref[0, hh] for hh in range(GLA_H)], axis=-1)
    o, s_new = _linear_attn_block(q, k_ref[...], v_ref[...], g, s_cat, p_ref, c, GLA_H, GLA_DV)
    for hh in range(GLA_H):
        vs = slice(hh * GLA_DV, (hh + 1) * GLA_DV)
        s_ref[0, hh] = s_new[:, vs]
        o_ref[:, vs] = _gated_head_out(o[hh * c:(hh + 1) * c], gate[:, vs],
                                       nw_ref[...]).astype(o_ref.dtype)


def gla_mixer(proj, small, wg_pad, bg, norm_w, state, b, t, c):
    m = proj.shape[0]
    nc = t // c
    kw = GLA_H * GLA_DK
    vw = GLA_H * GLA_DV
    p = jnp.asarray(_prefix_matrix(c), BF16)
    row = lambda bb, cc: bb * nc + cc
    kern = functools.partial(_gla_kernel, c=c)
    return pl.pallas_call(
        kern,
        grid=(b, nc),
        in_specs=[pl.BlockSpec((c, kw), lambda bb, cc: (row(bb, cc), 0)),
                  pl.BlockSpec((c, kw), lambda bb, cc: (row(bb, cc), 1)),
                  pl.BlockSpec((c, vw), lambda bb, cc: (row(bb, cc), 1)),
                  pl.BlockSpec((c, vw), lambda bb, cc: (row(bb, cc), 2)),
                  pl.BlockSpec((c, LANE), lambda bb, cc: (row(bb, cc), 0)),
                  pl.BlockSpec((LANE, kw), lambda bb, cc: (0, 0)),
                  pl.BlockSpec((1, kw), lambda bb, cc: (0, 0)),
                  pl.BlockSpec((1, GLA_DV), lambda bb, cc: (0, 0)),
                  pl.BlockSpec(p.shape, lambda bb, cc: (0, 0)),
                  pl.BlockSpec((1, GLA_H, GLA_DK, GLA_DV), lambda bb, cc: (bb, 0, 0, 0))],
        out_specs=[pl.BlockSpec((c, vw), lambda bb, cc: (row(bb, cc), 0)),
                   pl.BlockSpec((1, GLA_H, GLA_DK, GLA_DV), lambda bb, cc: (bb, 0, 0, 0))],
        out_shape=[jax.ShapeDtypeStruct((m, vw), BF16),
                   jax.ShapeDtypeStruct((b, GLA_H, GLA_DK, GLA_DV), F32)],
        compiler_params=_cparams(("parallel", "arbitrary")),
        name="gla_mixer",
    )(proj, proj, proj, proj, small, wg_pad, bg.reshape(1, -1), norm_w.reshape(1, -1), p, state)


HG_HB = 4


def _hgrn_kernel(q_ref, f_ref, i_ref, g_ref, lb_ref, nw_ref, p_ref, s0_ref,
                 o_ref, s_ref, *, c):
    ci = pl.program_id(2)

    @pl.when(ci == 0)
    def _():
        s_ref[...] = s0_ref[...]

    lb = lb_ref[...]
    log_lb = jnp.log(jnp.maximum(lb, TINY))
    zf = f_ref[...]
    a = log_lb
    cc = jnp.log1p(-lb) + _log_sigmoid(zf)
    log_f = jnp.maximum(a, cc) + jnp.log1p(jnp.exp(-jnp.abs(a - cc)))
    key = (1.0 - lb) * jax.nn.sigmoid(-zf)
    q = _silu(q_ref[...])
    v = i_ref[...]
    gate = g_ref[...]
    s_cat = jnp.concatenate([s_ref[0, hh] for hh in range(HG_HB)], axis=-1)
    o, s_new = _linear_attn_block(q, key, v, log_f, s_cat, p_ref, c, HG_HB, HG_DV)
    for hh in range(HG_HB):
        sl = slice(hh * HG_DV, (hh + 1) * HG_DV)
        s_ref[0, hh] = s_new[:, sl]
        o_ref[:, sl] = _gated_head_out(o[hh * c:(hh + 1) * c], gate[:, sl],
                                       nw_ref[...]).astype(o_ref.dtype)


def hgrn_mixer(proj, lb, norm_w, state, b, t, c):
    col0 = 0
    m = proj.shape[0]
    nc = t // c
    ng = HG_H // HG_HB
    w = HG_HB * HG_DK
    p = jnp.asarray(_prefix_matrix(c), BF16)
    row = lambda bb, g, cc: bb * nc + cc
    kern = functools.partial(_hgrn_kernel, c=c)
    return pl.pallas_call(
        kern,
        grid=(b, ng, nc),
        in_specs=[pl.BlockSpec((c, w), lambda bb, g, cc: (row(bb, g, cc), col0 + g)),
                  pl.BlockSpec((c, w), lambda bb, g, cc: (row(bb, g, cc), col0 + ng + g)),
                  pl.BlockSpec((c, w), lambda bb, g, cc: (row(bb, g, cc), col0 + 2 * ng + g)),
                  pl.BlockSpec((c, w), lambda bb, g, cc: (row(bb, g, cc), col0 + 3 * ng + g)),
                  pl.BlockSpec((1, w), lambda bb, g, cc: (0, g)),
                  pl.BlockSpec((1, HG_DV), lambda bb, g, cc: (0, 0)),
                  pl.BlockSpec(p.shape, lambda bb, g, cc: (0, 0)),
                  pl.BlockSpec((1, HG_HB, HG_DK, HG_DV), lambda bb, g, cc: (bb, g, 0, 0))],
        out_specs=[pl.BlockSpec((c, w), lambda bb, g, cc: (row(bb, g, cc), g)),
                   pl.BlockSpec((1, HG_HB, HG_DK, HG_DV), lambda bb, g, cc: (bb, g, 0, 0))],
        out_shape=[jax.ShapeDtypeStruct((m, HG_H * HG_DV), BF16),
                   jax.ShapeDtypeStruct((b, HG_H, HG_DK, HG_DV), F32)],
        compiler_params=_cparams(("parallel", "parallel", "arbitrary")),
        name="hgrn_mixer",
    )(proj, proj, proj, proj, lb.reshape(1, -1), norm_w.reshape(1, -1), p, state)


GDN_HB = 4
GDN_GS = 4


def _each(f, *lists):
    return [f(*args) for args in zip(*lists)]


def _unit_lower_inverse(ns, r, c):
    ri = lax.broadcasted_iota(jnp.int32, (r, r), 0)
    ci = lax.broadcasted_iota(jnp.int32, (r, r), 1)
    eye = (ri == ci).astype(F32)
    d = [jnp.where((ri >> 4) == (ci >> 4), n, 0.0) for n in ns]
    x = [eye - d_ for d_ in d]
    p = d
    for _ in range(3):
        p = _each(_mm, p, p)
        x = _each(lambda x_, p_: x_ + _mm(x_, p_), x, p)
    size = 16
    while size < c:
        sh = size.bit_length() - 1
        mask = ((ri >> (sh + 1)) == (ci >> (sh + 1))) & ((ri >> sh) > (ci >> sh))
        lx = _each(lambda n, x_: _mm(jnp.where(mask, n, 0.0), x_), ns, x)
        x = _each(lambda x_, lx_: x_ - _mm(x_, lx_), x, lx)
        size *= 2
    return x


def _gdn_blocks(q, k, v, beta, b_c, b_last, s_cat, c, nh):
    r = nh * c
    ri = lax.broadcasted_iota(jnp.int32, (r, r), 0)
    ci = lax.broadcasted_iota(jnp.int32, (r, r), 1)
    shc = c.bit_length() - 1
    causal = ((ri >> shc) == (ci >> shc)) & (ri >= ci)

    def decay_of(bc):
        b_r = jnp.sum(jnp.where(ri == ci, bc, 0.0), axis=0, keepdims=True)
        return jnp.where(causal, jnp.exp(jnp.minimum(bc - b_r, 0.0)), 0.0)

    decay = _each(decay_of, b_c)
    kb = [k_.astype(BF16) for k_ in k]
    kq = _each(lambda kb_, q_: _dot_nt(jnp.concatenate([kb_, q_.astype(BF16)], axis=0), kb_), kb, q)
    n = _each(lambda be, kq_, de: jnp.where(ri > ci, be * kq_[:r] * de, 0.0), beta, kq, decay)
    tinv = _unit_lower_inverse(n, r, c)
    eb = [jnp.exp(bc) for bc in b_c]
    sol = _each(lambda t_, v_, k_, be, e_: _mm(t_, jnp.concatenate([v_ * be, k_ * (be * e_)], axis=-1)),
                tinv, v, k, beta, eb)
    ws_qs = _each(lambda so, q_, e_, s_: _mm(jnp.concatenate([so[:, GDN_DV:], q_ * e_], axis=0), s_),
                  sol, q, eb, s_cat)
    v_new = _each(lambda so, wq: so[:, :GDN_DV] - _diag_blocks(wq[:r], nh, c, GDN_DV), sol, ws_qs)
    o = _each(lambda wq, kq_, de, vn: _diag_blocks(wq[r:], nh, c, GDN_DV) + _mm(kq_[r:] * de, vn),
              ws_qs, kq, decay, v_new)
    upd = _each(lambda k_, bl, bc, vn: _dot_tn((k_ * jnp.exp(bl - bc)).astype(BF16),
                                               _block_cols(vn, nh, c).astype(BF16)),
                k, b_last, b_c, v_new)

    def decayed(bl, s_, up):
        el = jnp.exp(bl)
        dec = jnp.concatenate(
            [jnp.broadcast_to(el[i * c:i * c + 1], (1, GDN_DV)) for i in range(nh)], axis=-1)
        return dec * s_ + up

    return o, _each(decayed, b_last, s_cat, upd)


def _gdn_kernel(q_ref, k_ref, v_ref, z_ref, gc_ref, wq_ref, wk_ref, wv_ref,
                pc_ref, nw_ref, tril_ref, s0_ref, cq0_ref, ck0_ref, cv0_ref,
                o_ref, s_ref, cq_ref, ck_ref, cv_ref,
                xq, xk, xv, *, c, nc):
    ci = pl.program_id(2)
    last = GDN_CONV - 1

    @pl.when(ci == 0)
    def _():
        s_ref[...] = s0_ref[...]

    def conv(x_ref, w_ref, c0_ref, c_ref, scr):
        @pl.when(ci == 0)
        def _():
            scr[8 - last:8, :] = c0_ref[0]

        scr[8:8 + c, :] = x_ref[...]
        y = scr[8:8 + c, :] * w_ref[last:last + 1, :]
        for j in range(last):
            y = y + scr[8 - last + j:8 - last + j + c, :] * w_ref[j:j + 1, :]
        tail = scr[8 + c - last:8 + c, :]
        scr[8 - last:8, :] = tail

        @pl.when(ci == nc - 1)
        def _():
            c_ref[0] = tail

        return _silu(y)

    qa = conv(q_ref, wq_ref, cq0_ref, cq_ref, xq)
    ka = conv(k_ref, wk_ref, ck0_ref, ck_ref, xk)
    va = conv(v_ref, wv_ref, cv0_ref, cv_ref, xv)

    nh = GDN_HB
    gw = nh * GDN_DK
    q, k, v, beta, b_c, b_last, s_cat = [], [], [], [], [], [], []
    for gi in range(GDN_GS):
        gcol = gc_ref[gi]
        pc = pc_ref[gi]
        beta_all = jax.nn.sigmoid(gcol)
        g_c = pc[0:1, :] * _softplus(gcol + pc[1:2, :])
        b_col = _sel_dot(tril_ref[...], g_c)
        beta.append(jnp.concatenate([beta_all[:, hh:hh + 1] for hh in range(nh)], axis=0))
        b_c.append(jnp.concatenate([b_col[:, nh + hh:nh + hh + 1] for hh in range(nh)], axis=0))
        b_last.append(jnp.concatenate(
            [jnp.broadcast_to(b_col[c - 1:c, nh + hh:nh + hh + 1], (c, 1)) for hh in range(nh)],
            axis=0))
        gs = slice(gi * gw, (gi + 1) * gw)
        qg = _stack_heads(qa[:, gs], nh, GDN_DK)
        kg = _stack_heads(ka[:, gs], nh, GDN_DK)
        q.append(qg * lax.rsqrt(jnp.sum(qg * qg, axis=-1, keepdims=True) + EPS) * (GDN_DK ** -0.5))
        k.append(kg * lax.rsqrt(jnp.sum(kg * kg, axis=-1, keepdims=True) + EPS))
        v.append(_stack_heads(va[:, gs], nh, GDN_DV))
        s_cat.append(jnp.concatenate([s_ref[0, gi * nh + hh] for hh in range(nh)], axis=-1))
    o, s_new = _gdn_blocks(q, k, v, beta, b_c, b_last, s_cat, c, nh)
    z = z_ref[...]
    for gi in range(GDN_GS):
        for hh in range(nh):
            sl = slice(hh * GDN_DV, (hh + 1) * GDN_DV)
            zl = slice(gi * gw + hh * GDN_DV, gi * gw + (hh + 1) * GDN_DV)
            s_ref[0, gi * nh + hh] = s_new[gi][:, sl]
            o_ref[:, zl] = _gated_head_out(o[gi][hh * c:(hh + 1) * c], z[:, zl],
                                           nw_ref[...]).astype(o_ref.dtype)


def gdn_mixer(proj, gates_c, conv_w, par_c, norm_w, state, cache, b, t, c):
    m = proj.shape[0]
    nc = t // c
    hs = GDN_HB * GDN_GS
    ng = GDN_H // hs
    w = hs * GDN_DK
    col0 = 0
    last = GDN_CONV - 1
    tril = jnp.asarray(np.tril(np.ones((c, c), np.float32)), BF16)
    row = lambda bb, g, cc: bb * nc + cc
    kern = functools.partial(_gdn_kernel, c=c, nc=nc)
    big = lambda off: pl.BlockSpec((c, w), lambda bb, g, cc: (row(bb, g, cc), col0 + off * ng + g))
    cw = lambda off: pl.BlockSpec((GDN_CONV, w), lambda bb, g, cc: (0, off * ng + g))
    c0 = lambda off: pl.BlockSpec((1, last, w), lambda bb, g, cc: (bb, 0, off * ng + g))
    cout = pl.BlockSpec((1, last, w), lambda bb, g, cc: (bb, 0, g))
    outs = pl.pallas_call(
        kern,
        grid=(b, ng, nc),
        in_specs=[big(0), big(1), big(2), big(3),
                  pl.BlockSpec((GDN_GS, c, LANE), lambda bb, g, cc: (g, row(bb, g, cc), 0)),
                  cw(0), cw(1), cw(2),
                  pl.BlockSpec((GDN_GS, 2, LANE), lambda bb, g, cc: (g, 0, 0)),
                  pl.BlockSpec((1, GDN_DV), lambda bb, g, cc: (0, 0)),
                  pl.BlockSpec((c, c), lambda bb, g, cc: (0, 0)),
                  pl.BlockSpec((1, hs, GDN_DK, GDN_DV), lambda bb, g, cc: (bb, g, 0, 0)),
                  c0(0), c0(1), c0(2)],
        out_specs=[pl.BlockSpec((c, w), lambda bb, g, cc: (row(bb, g, cc), g)),
                   pl.BlockSpec((1, hs, GDN_DK, GDN_DV), lambda bb, g, cc: (bb, g, 0, 0)),
                   cout, cout, cout],
        out_shape=[jax.ShapeDtypeStruct((m, GDN_H * GDN_DV), BF16),
                   jax.ShapeDtypeStruct((b, GDN_H, GDN_DK, GDN_DV), F32),
                   jax.ShapeDtypeStruct((b, last, GDN_H * GDN_DK), F32),
                   jax.ShapeDtypeStruct((b, last, GDN_H * GDN_DK), F32),
                   jax.ShapeDtypeStruct((b, last, GDN_H * GDN_DV), F32)],
        scratch_shapes=[pltpu.VMEM((8 + c, w), F32), pltpu.VMEM((8 + c, w), F32),
                        pltpu.VMEM((8 + c, w), F32)],
        compiler_params=_cparams(("parallel", "parallel", "arbitrary")),
        name="gdn_mixer",
    )(proj, proj, proj, proj, gates_c, conv_w, conv_w, conv_w, par_c,
      norm_w.reshape(1, -1), tril, state, cache, cache, cache)
    o, s, cq, ck, cv = outs
    return o, s, jnp.concatenate([cq, ck, cv], axis=-1)


def _row_tile(m, cap):
    tm = min(m, cap)
    while m % tm:
        tm //= 2
    return tm


def _group_gates(small):
    m = small.shape[0]
    ng = GDN_H // GDN_HB
    db = small[:, GLA_RANK:GLA_RANK + GDN_H].reshape(m, ng, GDN_HB)
    da = small[:, GLA_RANK + GDN_H:GLA_RANK + 2 * GDN_H].reshape(m, ng, GDN_HB)
    g = jnp.concatenate([db, da, jnp.zeros((m, ng, LANE - 2 * GDN_HB), F32)], axis=-1)
    return jnp.transpose(g, (1, 0, 2))


def _group_params(a_log, dt_bias):
    ng = GDN_H // GDN_HB
    na = -jnp.exp(a_log.astype(F32)).reshape(ng, GDN_HB)
    dt = dt_bias.astype(F32).reshape(ng, GDN_HB)
    pad = jnp.zeros((ng, GDN_HB), F32)
    tail = jnp.zeros((ng, LANE - 2 * GDN_HB), F32)
    return jnp.stack([jnp.concatenate([pad, na, tail], axis=-1),
                      jnp.concatenate([pad, dt, tail], axis=-1)], axis=1)


COL_GLA = 0
N_GLA = 2 * GLA_H * GLA_DK + 2 * GLA_H * GLA_DV
COL_LR = COL_GLA + N_GLA
COL_GDN = COL_LR + GLA_RANK
N_GDN = 2 * GDN_H * GDN_DK + 2 * GDN_H * GDN_DV
COL_BA = COL_GDN + N_GDN
COL_HG = COL_BA + 2 * GDN_H
N_HG = 2 * HG_H * HG_DK + 2 * HG_H * HG_DV

COL_TILE = 256
DOWN_TILES = (1024, 512)


def _prep_weights(w_gla_gate, w_down):
    wg_pad = jnp.pad(w_gla_gate.astype(BF16), ((0, 0), (0, LANE - GLA_RANK), (0, 0)))
    return wg_pad, w_down.astype(BF16)


def _trunk(x, pe, st_gla, st_gdn, cb_gdn, st_hg, cb_ffn, wts, prm, lower_bounds, norm_final):
    (w_in_t, wg_pad, w_out, w_up, w_down, w_pg, w_pp) = wts
    (norm_mix, b_gla_gate, gla_norm, w_gdn_conv, gdn_a_log, gdn_dt_bias, gdn_norm, hgrn_norm,
     norm_ffn, w_ffn_conv, norm_ple) = prm
    b, t, d = x.shape
    depth = w_in_t.shape[0]
    m = b * t
    c = min(MAX_CHUNK, t)
    tm = _row_tile(m, ROW_TILE)
    tr = _row_tile(t, 256)
    tn = COL_TILE
    f = w_down.shape[1]
    h = x.reshape(m, d)
    n_gla, n_gdn, n_gconv, n_hg, n_fconv = [], [], [], [], []
    for li in range(depth):
        xn = rmsnorm(h, norm_mix[li], BF16, _row_tile(m, 256))
        p_gla = matmul_t(xn, w_in_t, li, F32, tm, tn, COL_GLA, N_GLA)
        p_gdn = matmul_t(xn, w_in_t, li, F32, tm, tn, COL_GDN, N_GDN)
        p_hg = matmul_t(xn, w_in_t, li, F32, tm, tn, COL_HG, N_HG)
        small = gate_proj(xn, w_in_t, li, _row_tile(m, 1024), COL_LR, GLA_RANK, COL_BA, 2 * GDN_H)

        o_gla, s_gla = gla_mixer(p_gla, small, wg_pad[li], b_gla_gate[li], gla_norm[li],
                                 st_gla[li], b, t, c)
        o_gdn, s_gdn, b_gdn = gdn_mixer(p_gdn, _group_gates(small), w_gdn_conv[li],
                                        _group_params(gdn_a_log[li], gdn_dt_bias[li]), gdn_norm[li],
                                        st_gdn[li], cb_gdn[li], b, t, c)
        o_hg, s_hg = hgrn_mixer(p_hg, lower_bounds[li], hgrn_norm[li], st_hg[li], b, t, c)
        mix = jnp.concatenate([o_gla, o_gdn, o_hg], axis=-1)
        h = matmul_residual(mix, w_out, li, h, tm, tn, mix.shape[1])

        xn = rmsnorm(h, norm_ffn[li], BF16, _row_tile(m, 256))
        if t == tm:
            hid, cg, cv = ffn_up_fused(xn, w_up, li, w_ffn_conv[li], cb_ffn[li], tn)
        else:
            up = matmul(xn, w_up, li, F32, tm, tn)
            hid, cg, cv = ffn_conv_act(up, w_ffn_conv[li], cb_ffn[li], b, t, tr, f // 2)
        b_ffn = jnp.concatenate([cg, cv], axis=-1)
        h = matmul_residual(hid, w_down, li, h, _row_tile(m, DOWN_TILES[0]), DOWN_TILES[1], f // 2)

        xn = rmsnorm(h, norm_ple[li], BF16, _row_tile(m, 256))
        h = ple_update(xn, w_pg, pe[li].reshape(m, -1).astype(BF16), w_pp, li, h, tm, tn)

        n_gla.append(s_gla)
        n_gdn.append(s_gdn)
        n_gconv.append(b_gdn)
        n_hg.append(s_hg)
        n_fconv.append(b_ffn)
    y = rmsnorm(h, norm_final, F32, _row_tile(m, 256)).reshape(b, t, d)
    return (y, jnp.stack(n_gla), jnp.stack(n_gdn), jnp.stack(n_gconv),
            jnp.stack(n_hg), jnp.stack(n_fconv))


def kernel(x_prompt, x_sample, p_prompt, p_sample, state_gla, state_gdn, cache_gdn_conv, state_hgrn, cache_ffn_conv, norm_mix, w_in, w_gla_gate, b_gla_gate, gla_norm, w_gdn_conv, gdn_a_log, gdn_dt_bias, gdn_norm, hgrn_lb, hgrn_norm, w_out, norm_ffn, w_up, w_ffn_conv, w_down, norm_ple, w_ple_gate, w_ple_proj, norm_final):
    depth = w_in.shape[0]
    bp = x_prompt.shape[0]
    wg_pad, w_down_b = _prep_weights(w_gla_gate, w_down)
    wts = (jnp.swapaxes(w_in, 1, 2), wg_pad, w_out, w_up, w_down_b, w_ple_gate, w_ple_proj)
    prm = (norm_mix, b_gla_gate, gla_norm, w_gdn_conv, gdn_a_log, gdn_dt_bias, gdn_norm,
           hgrn_norm, norm_ffn, w_ffn_conv, norm_ple)
    sm = jax.nn.softmax(hgrn_lb.astype(F32), axis=0)
    lower_bounds = jnp.cumsum(sm, axis=0) - sm[0]
    zeros = lambda a: jnp.zeros((depth, bp) + a.shape[2:], F32)
    prompt = _trunk(x_prompt, p_prompt, zeros(state_gla), zeros(state_gdn), zeros(cache_gdn_conv),
                    zeros(state_hgrn), zeros(cache_ffn_conv), wts, prm, lower_bounds, norm_final)
    sample = _trunk(x_sample, p_sample, state_gla, state_gdn, cache_gdn_conv, state_hgrn,
                    cache_ffn_conv, wts, prm, lower_bounds, norm_final)
    return (prompt[0], sample[0]) + prompt[1:] + sample[1:]
```

```python
import functools

import numpy as np
import jax
import jax.numpy as jnp
from jax import lax
from jax.experimental import pallas as pl
from jax.experimental.pallas import tpu as pltpu

F32 = jnp.float32
BF16 = jnp.bfloat16

EPS = 1e-6
TINY = 1e-30
GLA_GATE_NORM = 16.0
GLA_H, GLA_DK, GLA_DV, GLA_RANK = 4, 128, 256, 16
GDN_H, GDN_DK, GDN_DV, GDN_CONV = 16, 128, 128, 4
HG_H, HG_DK, HG_DV = 8, 128, 128
FFN_CONV = 3
MAX_CHUNK = 64
ROW_TILE = 2048

LANE = 128
VMEM_LIMIT = 60 * 1024 * 1024


def _cparams(sem):
    return pltpu.CompilerParams(dimension_semantics=sem, vmem_limit_bytes=VMEM_LIMIT)


def _dot(a, b):
    return jnp.dot(a, b, preferred_element_type=F32)


def _dot_nt(a, b):
    return lax.dot_general(a, b, (((1,), (1,)), ((), ())), preferred_element_type=F32)


def _dot_tn(a, b):
    return lax.dot_general(a, b, (((0,), (0,)), ((), ())), preferred_element_type=F32)


def _mm(a, b):
    return _dot(a.astype(BF16), b.astype(BF16))


def _split3(x):
    hi = x.astype(BF16)
    r = x - hi.astype(F32)
    mid = r.astype(BF16)
    lo = (r - mid.astype(F32)).astype(BF16)
    return hi, mid, lo


def _sel_dot(p, x):
    hi, mid, lo = _split3(x)
    return _dot(p, hi) + _dot(p, mid) + _dot(p, lo)


def _dot_sel(x, p):
    hi, mid, lo = _split3(x)
    return _dot(hi, p) + _dot(mid, p) + _dot(lo, p)


def _softplus(x):
    return jnp.maximum(x, 0.0) + jnp.log1p(jnp.exp(-jnp.abs(x)))


def _log_sigmoid(x):
    return -_softplus(-x)


def _silu(x):
    return x * jax.nn.sigmoid(x)


def _rmsnorm_kernel(x_ref, w_ref, o_ref):
    x = x_ref[...]
    y = x * lax.rsqrt(jnp.mean(x * x, axis=-1, keepdims=True) + EPS)
    o_ref[...] = (y * w_ref[...]).astype(o_ref.dtype)


def rmsnorm(x, w, out_dtype, tm):
    m, d = x.shape
    return pl.pallas_call(
        _rmsnorm_kernel,
        grid=(m // tm,),
        in_specs=[pl.BlockSpec((tm, d), lambda i: (i, 0)),
                  pl.BlockSpec((1, d), lambda i: (0, 0))],
        out_specs=pl.BlockSpec((tm, d), lambda i: (i, 0)),
        out_shape=jax.ShapeDtypeStruct((m, d), out_dtype),
        compiler_params=_cparams(("parallel",)),
        name="rmsnorm",
    )(x, w.reshape(1, d))


SINGLE_BUFFER_BYTES = 16 * 1024 * 1024


def _act_spec(shape, index_map, tight=False, itemsize=2):
    if tight and shape[0] * shape[1] * itemsize >= SINGLE_BUFFER_BYTES:
        return pl.BlockSpec(shape, index_map, pipeline_mode=pl.Buffered(1))
    return pl.BlockSpec(shape, index_map)


def _mm_kernel(x_ref, w_ref, o_ref):
    o_ref[...] = _dot(x_ref[...], w_ref[...].astype(BF16)).astype(o_ref.dtype)


def matmul(x, w, li, out_dtype, tm, tn):
    m, k = x.shape
    n = w.shape[2]
    return pl.pallas_call(
        _mm_kernel,
        grid=(m // tm, n // tn),
        in_specs=[_act_spec((tm, k), lambda i, j: (i, 0)),
                  pl.BlockSpec((None, k, tn), lambda i, j: (li, 0, j))],
        out_specs=pl.BlockSpec((tm, tn), lambda i, j: (i, j)),
        out_shape=jax.ShapeDtypeStruct((m, n), out_dtype),
        compiler_params=_cparams(("parallel", "arbitrary")),
        name="matmul",
    )(x, w)


def _window_spec(rows, k, index_map):
    return pl.BlockSpec((pl.Element(1), pl.Element(rows), pl.Element(k)), index_map)


def _mm_t_kernel(x_ref, wt_ref, o_ref):
    o_ref[...] = _dot_nt(x_ref[...], wt_ref[0].astype(BF16)).astype(o_ref.dtype)


def matmul_t(x, wt, li, out_dtype, tm, tn, row0, nrows):
    m, k = x.shape
    assert row0 % 8 == 0 and nrows % tn == 0
    return pl.pallas_call(
        _mm_t_kernel,
        grid=(m // tm, nrows // tn),
        in_specs=[_act_spec((tm, k), lambda i, j: (i, 0)),
                  _window_spec(tn, k, lambda i, j: (li, (row0 // 8 + j * (tn // 8)) * 8, 0))],
        out_specs=pl.BlockSpec((tm, tn), lambda i, j: (i, j)),
        out_shape=jax.ShapeDtypeStruct((m, nrows), out_dtype),
        compiler_params=_cparams(("parallel", "arbitrary")),
        name="matmul_t",
    )(x, wt)


def _gate_proj_kernel(x_ref, wa_ref, wb_ref, o_ref):
    x = x_ref[...]
    a = _dot_nt(x, wa_ref[0].astype(BF16))
    b = _dot_nt(x, wb_ref[0].astype(BF16))
    pad = jnp.zeros((x.shape[0], LANE - a.shape[1] - b.shape[1]), F32)
    o_ref[...] = jnp.concatenate([a, b, pad], axis=-1)


def gate_proj(x, wt, li, tm, row_a, n_a, row_b, n_b):
    m, k = x.shape
    assert row_a % 8 == 0 and row_b % 8 == 0
    return pl.pallas_call(
        _gate_proj_kernel,
        grid=(m // tm,),
        in_specs=[pl.BlockSpec((tm, k), lambda i: (i, 0)),
                  _window_spec(n_a, k, lambda i: (li, row_a, 0)),
                  _window_spec(n_b, k, lambda i: (li, row_b, 0))],
        out_specs=pl.BlockSpec((tm, LANE), lambda i: (i, 0)),
        out_shape=jax.ShapeDtypeStruct((m, LANE), F32),
        compiler_params=_cparams(("parallel",)),
        name="gate_proj",
    )(x, wt, wt)


def _mm_res_kernel(x_ref, w_ref, r_ref, o_ref):
    k = pl.program_id(2)

    @pl.when(k == 0)
    def _():
        o_ref[...] = r_ref[...]

    o_ref[...] += _dot(x_ref[...], w_ref[...].astype(BF16))


def matmul_residual(x, w, li, res, tm, tn, tk):
    m, k = x.shape
    n = w.shape[2]
    return pl.pallas_call(
        _mm_res_kernel,
        grid=(m // tm, n // tn, k // tk),
        in_specs=[_act_spec((tm, tk), lambda i, j, kk: (i, kk)) if k == tk
                  else pl.BlockSpec((tm, tk), lambda i, j, kk: (i, kk)),
                  pl.BlockSpec((None, tk, tn), lambda i, j, kk: (li, kk, j)),
                  pl.BlockSpec((tm, tn), lambda i, j, kk: (i, j))],
        out_specs=pl.BlockSpec((tm, tn), lambda i, j, kk: (i, j)),
        out_shape=jax.ShapeDtypeStruct((m, n), F32),
        compiler_params=_cparams(("parallel", "arbitrary", "arbitrary")),
        name="matmul_residual",
    )(x, w, res)


def _ple_kernel(x_ref, wg_ref, p_ref, wp_ref, r_ref, o_ref):
    gate = jax.nn.sigmoid(_dot(x_ref[...], wg_ref[...].astype(BF16)))
    o_ref[...] = r_ref[...] + gate * _dot(p_ref[...], wp_ref[...].astype(BF16))


def ple_update(xn, wg, pe, wp, li, res, tm, tn):
    m, k = xn.shape
    n = wg.shape[2]
    kp = pe.shape[1]
    return pl.pallas_call(
        _ple_kernel,
        grid=(m // tm, n // tn),
        in_specs=[_act_spec((tm, k), lambda i, j: (i, 0)),
                  pl.BlockSpec((None, k, tn), lambda i, j: (li, 0, j)),
                  pl.BlockSpec((tm, kp), lambda i, j: (i, 0)),
                  pl.BlockSpec((None, kp, tn), lambda i, j: (li, 0, j)),
                  pl.BlockSpec((tm, tn), lambda i, j: (i, j))],
        out_specs=pl.BlockSpec((tm, tn), lambda i, j: (i, j)),
        out_shape=jax.ShapeDtypeStruct((m, n), F32),
        compiler_params=_cparams(("parallel", "arbitrary")),
        name="ple_update",
    )(xn, wg, pe, wp, res)


def _ffn_act_kernel(g_ref, v_ref, wg_ref, wv_ref, cg0_ref, cv0_ref,
                    h_ref, cg_ref, cv_ref, sg, sv, *, tr, n_row_tiles):
    i = pl.program_id(2)
    last = FFN_CONV - 1

    def conv(x_ref, w_ref, c0_ref, c_ref, scr):
        @pl.when(i == 0)
        def _():
            scr[8 - last:8, :] = c0_ref[0]

        scr[8:8 + tr, :] = x_ref[...]
        y = scr[8:8 + tr, :] * w_ref[last:last + 1, :]
        for j in range(last):
            y = y + scr[8 - last + j:8 - last + j + tr, :] * w_ref[j:j + 1, :]
        tail = scr[8 + tr - last:8 + tr, :]
        scr[8 - last:8, :] = tail

        @pl.when(i == n_row_tiles - 1)
        def _():
            c_ref[0] = tail

        return y

    gate = conv(g_ref, wg_ref, cg0_ref, cg_ref, sg)
    val = conv(v_ref, wv_ref, cv0_ref, cv_ref, sv)
    h_ref[...] = (_silu(gate) * val).astype(h_ref.dtype)


def ffn_conv_act(up, w_conv, cache, b, t, tr, tc):
    m, f2 = up.shape
    f = f2 // 2
    nj = f // tc
    n_row_tiles = t // tr
    last = FFN_CONV - 1
    kern = functools.partial(_ffn_act_kernel, tr=tr, n_row_tiles=n_row_tiles)
    return pl.pallas_call(
        kern,
        grid=(b, nj, n_row_tiles),
        in_specs=[pl.BlockSpec((tr, tc), lambda bb, j, i: (bb * n_row_tiles + i, j)),
                  pl.BlockSpec((tr, tc), lambda bb, j, i: (bb * n_row_tiles + i, nj + j)),
                  pl.BlockSpec((FFN_CONV, tc), lambda bb, j, i: (0, j)),
                  pl.BlockSpec((FFN_CONV, tc), lambda bb, j, i: (0, nj + j)),
                  pl.BlockSpec((1, last, tc), lambda bb, j, i: (bb, 0, j)),
                  pl.BlockSpec((1, last, tc), lambda bb, j, i: (bb, 0, nj + j))],
        out_specs=[pl.BlockSpec((tr, tc), lambda bb, j, i: (bb * n_row_tiles + i, j)),
                   pl.BlockSpec((1, last, tc), lambda bb, j, i: (bb, 0, j)),
                   pl.BlockSpec((1, last, tc), lambda bb, j, i: (bb, 0, j))],
        out_shape=[jax.ShapeDtypeStruct((m, f), BF16),
                   jax.ShapeDtypeStruct((b, last, f), F32),
                   jax.ShapeDtypeStruct((b, last, f), F32)],
        scratch_shapes=[pltpu.VMEM((8 + tr, tc), F32), pltpu.VMEM((8 + tr, tc), F32)],
        compiler_params=_cparams(("parallel", "parallel", "arbitrary")),
        name="ffn_conv_act",
    )(up, up, w_conv, w_conv, cache, cache)


def _ffn_up_kernel(x_ref, wg_ref, wv_ref, cwg_ref, cwv_ref, cg0_ref, cv0_ref,
                   h_ref, cg_ref, cv_ref, *, tm):
    last = FFN_CONV - 1
    x = x_ref[...]
    row = lax.broadcasted_iota(jnp.int32, (8, 1), 0)

    def conv(w_ref, cw_ref, c0_ref, c_ref):
        up = _dot(x, w_ref[...].astype(BF16))
        c_ref[0] = up[tm - last:tm, :]
        c0 = c0_ref[0]
        y = up * cw_ref[last:last + 1, :]
        for s in range(1, last + 1):
            sh = pltpu.roll(up, s, axis=0)
            head = sh[0:8]
            for r in range(s):
                head = jnp.where(row == r, c0[last - s + r:last - s + r + 1, :], head)
            y = y + jnp.concatenate([head, sh[8:]], axis=0) * cw_ref[last - s:last - s + 1, :]
        return y

    gate = conv(wg_ref, cwg_ref, cg0_ref, cg_ref)
    val = conv(wv_ref, cwv_ref, cv0_ref, cv_ref)
    h_ref[...] = (_silu(gate) * val).astype(h_ref.dtype)


def ffn_up_fused(xn, w_up, li, w_conv, cache, tn):
    m, d = xn.shape
    b = cache.shape[0]
    tm = m // b
    f = w_up.shape[2] // 2
    nj = f // tn
    last = FFN_CONV - 1
    kern = functools.partial(_ffn_up_kernel, tm=tm)
    cspec = lambda off: pl.BlockSpec((1, last, tn), lambda i, j: (i, 0, off + j))
    return pl.pallas_call(
        kern,
        grid=(b, nj),
        in_specs=[_act_spec((tm, d), lambda i, j: (i, 0), tight=True),
                  pl.BlockSpec((None, d, tn), lambda i, j: (li, 0, j)),
                  pl.BlockSpec((None, d, tn), lambda i, j: (li, 0, nj + j)),
                  pl.BlockSpec((FFN_CONV, tn), lambda i, j: (0, j)),
                  pl.BlockSpec((FFN_CONV, tn), lambda i, j: (0, nj + j)),
                  cspec(0), cspec(nj)],
        out_specs=[pl.BlockSpec((tm, tn), lambda i, j: (i, j)), cspec(0), cspec(0)],
        out_shape=[jax.ShapeDtypeStruct((m, f), BF16),
                   jax.ShapeDtypeStruct((b, last, f), F32),
                   jax.ShapeDtypeStruct((b, last, f), F32)],
        compiler_params=_cparams(("parallel", "arbitrary")),
        name="ffn_up_fused",
    )(xn, w_up, w_up, w_conv, w_conv, cache, cache)


def _levels(c):
    out, h = [], c // 2
    while h >= 1:
        out.append(h)
        h //= 2
    return out


def _prefix_matrix(c):
    blocks = [np.tril(np.ones((c, c), np.float32))]
    for h in _levels(c):
        p = np.zeros((c, c), np.float32)
        for t in range(c):
            mid = (t // (2 * h)) * 2 * h + h
            if t % (2 * h) >= h:
                p[t, mid:t + 1] = 1.0
            else:
                p[t, t + 1:mid] = 1.0
        blocks.append(p)
    return np.concatenate(blocks, axis=0)


def _gated_head_out(o, gate, w):
    o = o * lax.rsqrt(jnp.mean(o * o, axis=-1, keepdims=True) + EPS) * w
    return o * _silu(gate)


def _stack_heads(x, nh, w):
    return jnp.concatenate([x[:, i * w:(i + 1) * w] for i in range(nh)], axis=0)


def _diag_blocks(y, nh, c, w):
    return jnp.concatenate([y[i * c:(i + 1) * c, i * w:(i + 1) * w] for i in range(nh)], axis=0)


def _block_cols(v, nh, c):
    head = lax.broadcasted_iota(jnp.int32, (nh * c, 1), 0) >> (c.bit_length() - 1)
    return jnp.concatenate([jnp.where(head == i, v, 0.0) for i in range(nh)], axis=-1)


def _row_to_col(row):
    n = row.shape[1]
    eye = (lax.broadcasted_iota(jnp.int32, (n, n), 0) == lax.broadcasted_iota(jnp.int32, (n, n), 1))
    return jnp.sum(jnp.where(eye, row, 0.0), axis=1, keepdims=True)


def _linear_attn_block(q, k, v, g, s_cat, p_ref, c, nh, dv):
    dk = q.shape[1] // nh
    r = nh * c
    z = _sel_dot(p_ref[...], g)
    b = z[0:c]
    row = lax.broadcasted_iota(jnp.int32, (c, 1), 0)
    ri = lax.broadcasted_iota(jnp.int32, (r, r), 0)
    ci = lax.broadcasted_iota(jnp.int32, (r, r), 1)
    qk = _stack_heads(q * k, nh, dk)
    attn = jnp.where(ri == ci, jnp.sum(qk, axis=-1, keepdims=True), 0.0)
    for li, h in enumerate(_levels(c)):
        e = jnp.exp(z[(li + 1) * c:(li + 2) * c])
        lower = (row & (2 * h - 1)) >= h
        ql = _stack_heads(jnp.where(lower, q * e, 0.0), nh, dk).astype(BF16)
        kl = _stack_heads(jnp.where(lower, 0.0, k * e), nh, dk).astype(BF16)
        sh = (2 * h).bit_length() - 1
        attn = attn + jnp.where((ri >> sh) == (ci >> sh), _dot_nt(ql, kl), 0.0)
    vs = _stack_heads(v, nh, dv)
    qe = _stack_heads(q * jnp.exp(b), nh, dk)
    o = _diag_blocks(_mm(qe, s_cat), nh, c, dv) + _mm(attn, vs)
    b_last = b[c - 1:c]
    kt = _stack_heads(k * jnp.exp(b_last - b), nh, dk).astype(BF16)
    upd = _dot_tn(kt, _block_cols(vs, nh, c).astype(BF16))
    el = jnp.exp(b_last)
    dec = jnp.concatenate(
        [jnp.broadcast_to(_row_to_col(el[:, i * dk:(i + 1) * dk]), (dk, dv)) for i in range(nh)], axis=-1)
    return o, dec * s_cat + upd


def _gla_kernel(q_ref, k_ref, v_ref, gg_ref, sm_ref, wg_ref, bg_ref, nw_ref, p_ref, s0_ref,
                o_ref, s_ref, *, c):
    ci = pl.program_id(1)

    @pl.when(ci == 0)
    def _():
        s_ref[...] = s0_ref[...]

    x = _dot_sel(sm_ref[...], wg_ref[...]) + bg_ref[...]
    g = _log_sigmoid(x) * (1.0 / GLA_GATE_NORM)
    q = q_ref[...] * (GLA_DK ** -0.5)
    gate = gg_ref[...]
    s_cat = jnp.concatenate([s_ref[0, hh] for hh in range(GLA_H)], axis=-1)
    o, s_new = _linear_attn_block(q, k_ref[...], v_ref[...], g, s_cat, p_ref, c, GLA_H, GLA_DV)
    for hh in range(GLA_H):
        vs = slice(hh * GLA_DV, (hh + 1) * GLA_DV)
        s_ref[0, hh] = s_new[:, vs]
        o_ref[:, vs] = _gated_head_out(o[hh * c:(hh + 1) * c], gate[:, vs],
                                       nw_ref[...]).astype(o_ref.dtype)


def gla_mixer(proj, small, wg_pad, bg, norm_w, state, b, t, c):
    m = proj.shape[0]
    nc = t // c
    kw = GLA_H * GLA_DK
    vw = GLA_H * GLA_DV
    p = jnp.asarray(_prefix_matrix(c), BF16)
    row = lambda bb, cc: bb * nc + cc
    kern = functools.partial(_gla_kernel, c=c)
    return pl.pallas_call(
        kern,
        grid=(b, nc),
        in_specs=[pl.BlockSpec((c, kw), lambda bb, cc: (row(bb, cc), 0)),
                  pl.BlockSpec((c, kw), lambda bb, cc: (row(bb, cc), 1)),
                  pl.BlockSpec((c, vw), lambda bb, cc: (row(bb, cc), 1)),
                  pl.BlockSpec((c, vw), lambda bb, cc: (row(bb, cc), 2)),
                  pl.BlockSpec((c, LANE), lambda bb, cc: (row(bb, cc), 0)),
                  pl.BlockSpec((LANE, kw), lambda bb, cc: (0, 0)),
                  pl.BlockSpec((1, kw), lambda bb, cc: (0, 0)),
                  pl.BlockSpec((1, GLA_DV), lambda bb, cc: (0, 0)),
                  pl.BlockSpec(p.shape, lambda bb, cc: (0, 0)),
                  pl.BlockSpec((1, GLA_H, GLA_DK, GLA_DV), lambda bb, cc: (bb, 0, 0, 0))],
        out_specs=[pl.BlockSpec((c, vw), lambda bb, cc: (row(bb, cc), 0)),
                   pl.BlockSpec((1, GLA_H, GLA_DK, GLA_DV), lambda bb, cc: (bb, 0, 0, 0))],
        out_shape=[jax.ShapeDtypeStruct((m, vw), BF16),
                   jax.ShapeDtypeStruct((b, GLA_H, GLA_DK, GLA_DV), F32)],
        compiler_params=_cparams(("parallel", "arbitrary")),
        name="gla_mixer",
    )(proj, proj, proj, proj, small, wg_pad, bg.reshape(1, -1), norm_w.reshape(1, -1), p, state)


HG_HB = 4


def _hgrn_kernel(q_ref, f_ref, i_ref, g_ref, lb_ref, nw_ref, p_ref, s0_ref,
                 o_ref, s_ref, *, c):
    ci = pl.program_id(2)

    @pl.when(ci == 0)
    def _():
        s_ref[...] = s0_ref[...]

    lb = lb_ref[...]
    log_lb = jnp.log(jnp.maximum(lb, TINY))
    zf = f_ref[...]
    a = log_lb
    cc = jnp.log1p(-lb) + _log_sigmoid(zf)
    log_f = jnp.maximum(a, cc) + jnp.log1p(jnp.exp(-jnp.abs(a - cc)))
    key = (1.0 - lb) * jax.nn.sigmoid(-zf)
    q = _silu(q_ref[...])
    v = i_ref[...]
    gate = g_ref[...]
    s_cat = jnp.concatenate([s_ref[0, hh] for hh in range(HG_HB)], axis=-1)
    o, s_new = _linear_attn_block(q, key, v, log_f, s_cat, p_ref, c, HG_HB, HG_DV)
    for hh in range(HG_HB):
        sl = slice(hh * HG_DV, (hh + 1) * HG_DV)
        s_ref[0, hh] = s_new[:, sl]
        o_ref[:, sl] = _gated_head_out(o[hh * c:(hh + 1) * c], gate[:, sl],
                                       nw_ref[...]).astype(o_ref.dtype)


def hgrn_mixer(proj, lb, norm_w, state, b, t, c):
    col0 = 0
    m = proj.shape[0]
    nc = t // c
    ng = HG_H // HG_HB
    w = HG_HB * HG_DK
    p = jnp.asarray(_prefix_matrix(c), BF16)
    row = lambda bb, g, cc: bb * nc + cc
    kern = functools.partial(_hgrn_kernel, c=c)
    return pl.pallas_call(
        kern,
        grid=(b, ng, nc),
        in_specs=[pl.BlockSpec((c, w), lambda bb, g, cc: (row(bb, g, cc), col0 + g)),
                  pl.BlockSpec((c, w), lambda bb, g, cc: (row(bb, g, cc), col0 + ng + g)),
                  pl.BlockSpec((c, w), lambda bb, g, cc: (row(bb, g, cc), col0 + 2 * ng + g)),
                  pl.BlockSpec((c, w), lambda bb, g, cc: (row(bb, g, cc), col0 + 3 * ng + g)),
                  pl.BlockSpec((1, w), lambda bb, g, cc: (0, g)),
                  pl.BlockSpec((1, HG_DV), lambda bb, g, cc: (0, 0)),
                  pl.BlockSpec(p.shape, lambda bb, g, cc: (0, 0)),
                  pl.BlockSpec((1, HG_HB, HG_DK, HG_DV), lambda bb, g, cc: (bb, g, 0, 0))],
        out_specs=[pl.BlockSpec((c, w), lambda bb, g, cc: (row(bb, g, cc), g)),
                   pl.BlockSpec((1, HG_HB, HG_DK, HG_DV), lambda bb, g, cc: (bb, g, 0, 0))],
        out_shape=[jax.ShapeDtypeStruct((m, HG_H * HG_DV), BF16),
                   jax.ShapeDtypeStruct((b, HG_H, HG_DK, HG_DV), F32)],
        compiler_params=_cparams(("parallel", "parallel", "arbitrary")),
        name="hgrn_mixer",
    )(proj, proj, proj, proj, lb.reshape(1, -1), norm_w.reshape(1, -1), p, state)


GDN_HB = 4
GDN_GS = 4


def _each(f, *lists):
    return [f(*args) for args in zip(*lists)]


def _unit_lower_inverse(ns, r, c):
    ri = lax.broadcasted_iota(jnp.int32, (r, r), 0)
    ci = lax.broadcasted_iota(jnp.int32, (r, r), 1)
    eye = (ri == ci).astype(F32)
    d = [jnp.where((ri >> 4) == (ci >> 4), n, 0.0) for n in ns]
    x = [eye - d_ for d_ in d]
    p = d
    for _ in range(3):
        p = _each(_mm, p, p)
        x = _each(lambda x_, p_: x_ + _mm(x_, p_), x, p)
    size = 16
    while size < c:
        sh = size.bit_length() - 1
        mask = ((ri >> (sh + 1)) == (ci >> (sh + 1))) & ((ri >> sh) > (ci >> sh))
        lx = _each(lambda n, x_: _mm(jnp.where(mask, n, 0.0), x_), ns, x)
        x = _each(lambda x_, lx_: x_ - _mm(x_, lx_), x, lx)
        size *= 2
    return x


def _gdn_blocks(q, k, v, beta, b_c, b_last, s_cat, c, nh):
    r = nh * c
    ri = lax.broadcasted_iota(jnp.int32, (r, r), 0)
    ci = lax.broadcasted_iota(jnp.int32, (r, r), 1)
    shc = c.bit_length() - 1
    causal = ((ri >> shc) == (ci >> shc)) & (ri >= ci)

    def decay_of(bc):
        b_r = jnp.sum(jnp.where(ri == ci, bc, 0.0), axis=0, keepdims=True)
        return jnp.where(causal, jnp.exp(jnp.minimum(bc - b_r, 0.0)), 0.0)

    decay = _each(decay_of, b_c)
    kb = [k_.astype(BF16) for k_ in k]
    kq = _each(lambda kb_, q_: _dot_nt(jnp.concatenate([kb_, q_.astype(BF16)], axis=0), kb_), kb, q)
    n = _each(lambda be, kq_, de: jnp.where(ri > ci, be * kq_[:r] * de, 0.0), beta, kq, decay)
    tinv = _unit_lower_inverse(n, r, c)
    eb = [jnp.exp(bc) for bc in b_c]
    sol = _each(lambda t_, v_, k_, be, e_: _mm(t_, jnp.concatenate([v_ * be, k_ * (be * e_)], axis=-1)),
                tinv, v, k, beta, eb)
    ws_qs = _each(lambda so, q_, e_, s_: _mm(jnp.concatenate([so[:, GDN_DV:], q_ * e_], axis=0), s_),
                  sol, q, eb, s_cat)
    v_new = _each(lambda so, wq: so[:, :GDN_DV] - _diag_blocks(wq[:r], nh, c, GDN_DV), sol, ws_qs)
    o = _each(lambda wq, kq_, de, vn: _diag_blocks(wq[r:], nh, c, GDN_DV) + _mm(kq_[r:] * de, vn),
              ws_qs, kq, decay, v_new)
    upd = _each(lambda k_, bl, bc, vn: _dot_tn((k_ * jnp.exp(bl - bc)).astype(BF16),
                                               _block_cols(vn, nh, c).astype(BF16)),
                k, b_last, b_c, v_new)

    def decayed(bl, s_, up):
        el = jnp.exp(bl)
        dec = jnp.concatenate(
            [jnp.broadcast_to(el[i * c:i * c + 1], (1, GDN_DV)) for i in range(nh)], axis=-1)
        return dec * s_ + up

    return o, _each(decayed, b_last, s_cat, upd)


def _gdn_kernel(q_ref, k_ref, v_ref, z_ref, gc_ref, wq_ref, wk_ref, wv_ref,
                pc_ref, nw_ref, tril_ref, s0_ref, cq0_ref, ck0_ref, cv0_ref,
                o_ref, s_ref, cq_ref, ck_ref, cv_ref,
                xq, xk, xv, *, c, nc):
    ci = pl.program_id(2)
    last = GDN_CONV - 1

    @pl.when(ci == 0)
    def _():
        s_ref[...] = s0_ref[...]

    def conv(x_ref, w_ref, c0_ref, c_ref, scr):
        @pl.when(ci == 0)
        def _():
            scr[8 - last:8, :] = c0_ref[0]

        scr[8:8 + c, :] = x_ref[...]
        y = scr[8:8 + c, :] * w_ref[last:last + 1, :]
        for j in range(last):
            y = y + scr[8 - last + j:8 - last + j + c, :] * w_ref[j:j + 1, :]
        tail = scr[8 + c - last:8 + c, :]
        scr[8 - last:8, :] = tail

        @pl.when(ci == nc - 1)
        def _():
            c_ref[0] = tail

        return _silu(y)

    qa = conv(q_ref, wq_ref, cq0_ref, cq_ref, xq)
    ka = conv(k_ref, wk_ref, ck0_ref, ck_ref, xk)
    va = conv(v_ref, wv_ref, cv0_ref, cv_ref, xv)

    nh = GDN_HB
    gw = nh * GDN_DK
    q, k, v, beta, b_c, b_last, s_cat = [], [], [], [], [], [], []
    for gi in range(GDN_GS):
        gcol = gc_ref[gi]
        pc = pc_ref[gi]
        beta_all = jax.nn.sigmoid(gcol)
        g_c = pc[0:1, :] * _softplus(gcol + pc[1:2, :])
        b_col = _sel_dot(tril_ref[...], g_c)
        beta.append(jnp.concatenate([beta_all[:, hh:hh + 1] for hh in range(nh)], axis=0))
        b_c.append(jnp.concatenate([b_col[:, nh + hh:nh + hh + 1] for hh in range(nh)], axis=0))
        b_last.append(jnp.concatenate(
            [jnp.broadcast_to(b_col[c - 1:c, nh + hh:nh + hh + 1], (c, 1)) for hh in range(nh)],
            axis=0))
        gs = slice(gi * gw, (gi + 1) * gw)
        qg = _stack_heads(qa[:, gs], nh, GDN_DK)
        kg = _stack_heads(ka[:, gs], nh, GDN_DK)
        q.append(qg * lax.rsqrt(jnp.sum(qg * qg, axis=-1, keepdims=True) + EPS) * (GDN_DK ** -0.5))
        k.append(kg * lax.rsqrt(jnp.sum(kg * kg, axis=-1, keepdims=True) + EPS))
        v.append(_stack_heads(va[:, gs], nh, GDN_DV))
        s_cat.append(jnp.concatenate([s_ref[0, gi * nh + hh] for hh in range(nh)], axis=-1))
    o, s_new = _gdn_blocks(q, k, v, beta, b_c, b_last, s_cat, c, nh)
    z = z_ref[...]
    for gi in range(GDN_GS):
        for hh in range(nh):
            sl = slice(hh * GDN_DV, (hh + 1) * GDN_DV)
            zl = slice(gi * gw + hh * GDN_DV, gi * gw + (hh + 1) * GDN_DV)
            s_ref[0, gi * nh + hh] = s_new[gi][:, sl]
            o_ref[:, zl] = _gated_head_out(o[gi][hh * c:(hh + 1) * c], z[:, zl],
                                           nw_ref[...]).astype(o_ref.dtype)


def gdn_mixer(proj, gates_c, conv_w, par_c, norm_w, state, cache, b, t, c):
    m = proj.shape[0]
    nc = t // c
    hs = GDN_HB * GDN_GS
    ng = GDN_H // hs
    w = hs * GDN_DK
    col0 = 0
    last = GDN_CONV - 1
    tril = jnp.asarray(np.tril(np.ones((c, c), np.float32)), BF16)
    row = lambda bb, g, cc: bb * nc + cc
    kern = functools.partial(_gdn_kernel, c=c, nc=nc)
    big = lambda off: pl.BlockSpec((c, w), lambda bb, g, cc: (row(bb, g, cc), col0 + off * ng + g))
    cw = lambda off: pl.BlockSpec((GDN_CONV, w), lambda bb, g, cc: (0, off * ng + g))
    c0 = lambda off: pl.BlockSpec((1, last, w), lambda bb, g, cc: (bb, 0, off * ng + g))
    cout = pl.BlockSpec((1, last, w), lambda bb, g, cc: (bb, 0, g))
    outs = pl.pallas_call(
        kern,
        grid=(b, ng, nc),
        in_specs=[big(0), big(1), big(2), big(3),
                  pl.BlockSpec((GDN_GS, c, LANE), lambda bb, g, cc: (g, row(bb, g, cc), 0)),
                  cw(0), cw(1), cw(2),
                  pl.BlockSpec((GDN_GS, 2, LANE), lambda bb, g, cc: (g, 0, 0)),
                  pl.BlockSpec((1, GDN_DV), lambda bb, g, cc: (0, 0)),
                  pl.BlockSpec((c, c), lambda bb, g, cc: (0, 0)),
                  pl.BlockSpec((1, hs, GDN_DK, GDN_DV), lambda bb, g, cc: (bb, g, 0, 0)),
                  c0(0), c0(1), c0(2)],
        out_specs=[pl.BlockSpec((c, w), lambda bb, g, cc: (row(bb, g, cc), g)),
                   pl.BlockSpec((1, hs, GDN_DK, GDN_DV), lambda bb, g, cc: (bb, g, 0, 0)),
                   cout, cout, cout],
        out_shape=[jax.ShapeDtypeStruct((m, GDN_H * GDN_DV), BF16),
                   jax.ShapeDtypeStruct((b, GDN_H, GDN_DK, GDN_DV), F32),
                   jax.ShapeDtypeStruct((b, last, GDN_H * GDN_DK), F32),
                   jax.ShapeDtypeStruct((b, last, GDN_H * GDN_DK), F32),
                   jax.ShapeDtypeStruct((b, last, GDN_H * GDN_DV), F32)],
        scratch_shapes=[pltpu.VMEM((8 + c, w), F32), pltpu.VMEM((8 + c, w), F32),
                        pltpu.VMEM((8 + c, w), F32)],
        compiler_params=_cparams(("parallel", "parallel", "arbitrary")),
        name="gdn_mixer",
    )(proj, proj, proj, proj, gates_c, conv_w, conv_w, conv_w, par_c,
      norm_w.reshape(1, -1), tril, state, cache, cache, cache)
    o, s, cq, ck, cv = outs
    return o, s, jnp.concatenate([cq, ck, cv], axis=-1)


def _row_tile(m, cap):
    tm = min(m, cap)
    while m % tm:
        tm //= 2
    return tm


def _group_gates(small):
    m = small.shape[0]
    ng = GDN_H // GDN_HB
    db = small[:, GLA_RANK:GLA_RANK + GDN_H].reshape(m, ng, GDN_HB)
    da = small[:, GLA_RANK + GDN_H:GLA_RANK + 2 * GDN_H].reshape(m, ng, GDN_HB)
    g = jnp.concatenate([db, da, jnp.zeros((m, ng, LANE - 2 * GDN_HB), F32)], axis=-1)
    return jnp.transpose(g, (1, 0, 2))


def _group_params(a_log, dt_bias):
    ng = GDN_H // GDN_HB
    na = -jnp.exp(a_log.astype(F32)).reshape(ng, GDN_HB)
    dt = dt_bias.astype(F32).reshape(ng, GDN_HB)
    pad = jnp.zeros((ng, GDN_HB), F32)
    tail = jnp.zeros((ng, LANE - 2 * GDN_HB), F32)
    return jnp.stack([jnp.concatenate([pad, na, tail], axis=-1),
                      jnp.concatenate([pad, dt, tail], axis=-1)], axis=1)


COL_GLA = 0
N_GLA = 2 * GLA_H * GLA_DK + 2 * GLA_H * GLA_DV
COL_LR = COL_GLA + N_GLA
COL_GDN = COL_LR + GLA_RANK
N_GDN = 2 * GDN_H * GDN_DK + 2 * GDN_H * GDN_DV
COL_BA = COL_GDN + N_GDN
COL_HG = COL_BA + 2 * GDN_H
N_HG = 2 * HG_H * HG_DK + 2 * HG_H * HG_DV

COL_TILE = 256
DOWN_TILES = (1024, 512)


def _prep_weights(w_gla_gate, w_down):
    wg_pad = jnp.pad(w_gla_gate.astype(BF16), ((0, 0), (0, LANE - GLA_RANK), (0, 0)))
    return wg_pad, w_down.astype(BF16)


def _trunk(x, pe, st_gla, st_gdn, cb_gdn, st_hg, cb_ffn, wts, prm, lower_bounds, norm_final):
    (w_in_t, wg_pad, w_out, w_up, w_down, w_pg, w_pp) = wts
    (norm_mix, b_gla_gate, gla_norm, w_gdn_conv, gdn_a_log, gdn_dt_bias, gdn_norm, hgrn_norm,
     norm_ffn, w_ffn_conv, norm_ple) = prm
    b, t, d = x.shape
    depth = w_in_t.shape[0]
    m = b * t
    c = min(MAX_CHUNK, t)
    tm = _row_tile(m, ROW_TILE)
    tr = _row_tile(t, 256)
    tn = COL_TILE
    f = w_down.shape[1]
    h = x.reshape(m, d)
    n_gla, n_gdn, n_gconv, n_hg, n_fconv = [], [], [], [], []
    for li in range(depth):
        xn = rmsnorm(h, norm_mix[li], BF16, _row_tile(m, 256))
        p_gla = matmul_t(xn, w_in_t, li, F32, tm, tn, COL_GLA, N_GLA)
        p_gdn = matmul_t(xn, w_in_t, li, F32, tm, tn, COL_GDN, N_GDN)
        p_hg = matmul_t(xn, w_in_t, li, F32, tm, tn, COL_HG, N_HG)
        small = gate_proj(xn, w_in_t, li, _row_tile(m, 1024), COL_LR, GLA_RANK, COL_BA, 2 * GDN_H)

        o_gla, s_gla = gla_mixer(p_gla, small, wg_pad[li], b_gla_gate[li], gla_norm[li],
                                 st_gla[li], b, t, c)
        o_gdn, s_gdn, b_gdn = gdn_mixer(p_gdn, _group_gates(small), w_gdn_conv[li],
                                        _group_params(gdn_a_log[li], gdn_dt_bias[li]), gdn_norm[li],
                                        st_gdn[li], cb_gdn[li], b, t, c)
        o_hg, s_hg = hgrn_mixer(p_hg, lower_bounds[li], hgrn_norm[li], st_hg[li], b, t, c)
        mix = jnp.concatenate([o_gla, o_gdn, o_hg], axis=-1)
        h = matmul_residual(mix, w_out, li, h, tm, tn, mix.shape[1])

        xn = rmsnorm(h, norm_ffn[li], BF16, _row_tile(m, 256))
        if t == tm:
            hid, cg, cv = ffn_up_fused(xn, w_up, li, w_ffn_conv[li], cb_ffn[li], tn)
        else:
            up = matmul(xn, w_up, li, F32, tm, tn)
            hid, cg, cv = ffn_conv_act(up, w_ffn_conv[li], cb_ffn[li], b, t, tr, f // 2)
        b_ffn = jnp.concatenate([cg, cv], axis=-1)
        h = matmul_residual(hid, w_down, li, h, _row_tile(m, DOWN_TILES[0]), DOWN_TILES[1], f // 2)

        xn = rmsnorm(h, norm_ple[li], BF16, _row_tile(m, 256))
        h = ple_update(xn, w_pg, pe[li].reshape(m, -1).astype(BF16), w_pp, li, h, tm, tn)

        n_gla.append(s_gla)
        n_gdn.append(s_gdn)
        n_gconv.append(b_gdn)
        n_hg.append(s_hg)
        n_fconv.append(b_ffn)
    y = rmsnorm(h, norm_final, F32, _row_tile(m, 256)).reshape(b, t, d)
    return (y, jnp.stack(n_gla), jnp.stack(n_gdn), jnp.stack(n_gconv),
            jnp.stack(n_hg), jnp.stack(n_fconv))


def kernel(x_prompt, x_sample, p_prompt, p_sample, state_gla, state_gdn, cache_gdn_conv, state_hgrn, cache_ffn_conv, norm_mix, w_in, w_gla_gate, b_gla_gate, gla_norm, w_gdn_conv, gdn_a_log, gdn_dt_bias, gdn_norm, hgrn_lb, hgrn_norm, w_out, norm_ffn, w_up, w_ffn_conv, w_down, norm_ple, w_ple_gate, w_ple_proj, norm_final):
    depth = w_in.shape[0]
    bp = x_prompt.shape[0]
    wg_pad, w_down_b = _prep_weights(w_gla_gate, w_down)
    wts = (jnp.swapaxes(w_in, 1, 2), wg_pad, w_out, w_up, w_down_b, w_ple_gate, w_ple_proj)
    prm = (norm_mix, b_gla_gate, gla_norm, w_gdn_conv, gdn_a_log, gdn_dt_bias, gdn_norm,
           hgrn_norm, norm_ffn, w_ffn_conv, norm_ple)
    sm = jax.nn.softmax(hgrn_lb.astype(F32), axis=0)
    lower_bounds = jnp.cumsum(sm, axis=0) - sm[0]
    zeros = lambda a: jnp.zeros((depth, bp) + a.shape[2:], F32)
    prompt = _trunk(x_prompt, p_prompt, zeros(state_gla), zeros(state_gdn), zeros(cache_gdn_conv),
                    zeros(state_hgrn), zeros(cache_ffn_conv), wts, prm, lower_bounds, norm_final)
    sample = _trunk(x_sample, p_sample, state_gla, state_gdn, cache_gdn_conv, state_hgrn,
                    cache_ffn_conv, wts, prm, lower_bounds, norm_final)
    return (prompt[0], sample[0]) + prompt[1:] + sample[1:]
```

```python
import functools

import numpy as np
import jax
import jax.numpy as jnp
from jax import lax
from jax.experimental import pallas as pl
from jax.experimental.pallas import tpu as pltpu

F32 = jnp.float32
BF16 = jnp.bfloat16

EPS = 1e-6
TINY = 1e-30
GLA_GATE_NORM = 16.0
GLA_H, GLA_DK, GLA_DV, GLA_RANK = 4, 128, 256, 16
GDN_H, GDN_DK, GDN_DV, GDN_CONV = 16, 128, 128, 4
HG_H, HG_DK, HG_DV = 8, 128, 128
FFN_CONV = 3
MAX_CHUNK = 64
ROW_TILE = 2048

LANE = 128
VMEM_LIMIT = 60 * 1024 * 1024


def _cparams(sem):
    return pltpu.CompilerParams(dimension_semantics=sem, vmem_limit_bytes=VMEM_LIMIT)


def _dot(a, b):
    return jnp.dot(a, b, preferred_element_type=F32)


def _dot_nt(a, b):
    return lax.dot_general(a, b, (((1,), (1,)), ((), ())), preferred_element_type=F32)


def _dot_tn(a, b):
    return lax.dot_general(a, b, (((0,), (0,)), ((), ())), preferred_element_type=F32)


def _mm(a, b):
    return _dot(a.astype(BF16), b.astype(BF16))


def _split3(x):
    hi = x.astype(BF16)
    r = x - hi.astype(F32)
    mid = r.astype(BF16)
    lo = (r - mid.astype(F32)).astype(BF16)
    return hi, mid, lo


def _sel_dot(p, x):
    hi, mid, lo = _split3(x)
    return _dot(p, hi) + _dot(p, mid) + _dot(p, lo)


def _dot_sel(x, p):
    hi, mid, lo = _split3(x)
    return _dot(hi, p) + _dot(mid, p) + _dot(lo, p)


def _softplus(x):
    return jnp.maximum(x, 0.0) + jnp.log1p(jnp.exp(-jnp.abs(x)))


def _log_sigmoid(x):
    return -_softplus(-x)


def _silu(x):
    return x * jax.nn.sigmoid(x)


def _rmsnorm_kernel(x_ref, w_ref, o_ref):
    x = x_ref[...]
    y = x * lax.rsqrt(jnp.mean(x * x, axis=-1, keepdims=True) + EPS)
    o_ref[...] = (y * w_ref[...]).astype(o_ref.dtype)


def rmsnorm(x, w, out_dtype, tm):
    m, d = x.shape
    return pl.pallas_call(
        _rmsnorm_kernel,
        grid=(m // tm,),
        in_specs=[pl.BlockSpec((tm, d), lambda i: (i, 0)),
                  pl.BlockSpec((1, d), lambda i: (0, 0))],
        out_specs=pl.BlockSpec((tm, d), lambda i: (i, 0)),
        out_shape=jax.ShapeDtypeStruct((m, d), out_dtype),
        compiler_params=_cparams(("parallel",)),
        name="rmsnorm",
    )(x, w.reshape(1, d))


SINGLE_BUFFER_BYTES = 16 * 1024 * 1024


def _act_spec(shape, index_map, tight=False, itemsize=2):
    if tight and shape[0] * shape[1] * itemsize >= SINGLE_BUFFER_BYTES:
        return pl.BlockSpec(shape, index_map, pipeline_mode=pl.Buffered(1))
    return pl.BlockSpec(shape, index_map)


def _mm_kernel(x_ref, w_ref, o_ref):
    o_ref[...] = _dot(x_ref[...], w_ref[...].astype(BF16)).astype(o_ref.dtype)


def matmul(x, w, li, out_dtype, tm, tn):
    m, k = x.shape
    n = w.shape[2]
    return pl.pallas_call(
        _mm_kernel,
        grid=(m // tm, n // tn),
        in_specs=[_act_spec((tm, k), lambda i, j: (i, 0)),
                  pl.BlockSpec((None, k, tn), lambda i, j: (li, 0, j))],
        out_specs=pl.BlockSpec((tm, tn), lambda i, j: (i, j)),
        out_shape=jax.ShapeDtypeStruct((m, n), out_dtype),
        compiler_params=_cparams(("parallel", "arbitrary")),
        name="matmul",
    )(x, w)


def _window_spec(rows, k, index_map):
    return pl.BlockSpec((pl.Element(1), pl.Element(rows), pl.Element(k)), index_map)


def _mm_t_kernel(x_ref, wt_ref, o_ref):
    o_ref[...] = _dot_nt(x_ref[...], wt_ref[0].astype(BF16)).astype(o_ref.dtype)


def matmul_t(x, wt, li, out_dtype, tm, tn, row0, nrows):
    m, k = x.shape
    assert row0 % 8 == 0 and nrows % tn == 0
    return pl.pallas_call(
        _mm_t_kernel,
        grid=(m // tm, nrows // tn),
        in_specs=[_act_spec((tm, k), lambda i, j: (i, 0)),
                  _window_spec(tn, k, lambda i, j: (li, (row0 // 8 + j * (tn // 8)) * 8, 0))],
        out_specs=pl.BlockSpec((tm, tn), lambda i, j: (i, j)),
        out_shape=jax.ShapeDtypeStruct((m, nrows), out_dtype),
        compiler_params=_cparams(("parallel", "arbitrary")),
        name="matmul_t",
    )(x, wt)


def _gate_proj_kernel(x_ref, wa_ref, wb_ref, o_ref):
    x = x_ref[...]
    a = _dot_nt(x, wa_ref[0].astype(BF16))
    b = _dot_nt(x, wb_ref[0].astype(BF16))
    pad = jnp.zeros((x.shape[0], LANE - a.shape[1] - b.shape[1]), F32)
    o_ref[...] = jnp.concatenate([a, b, pad], axis=-1)


def gate_proj(x, wt, li, tm, row_a, n_a, row_b, n_b):
    m, k = x.shape
    assert row_a % 8 == 0 and row_b % 8 == 0
    return pl.pallas_call(
        _gate_proj_kernel,
        grid=(m // tm,),
        in_specs=[pl.BlockSpec((tm, k), lambda i: (i, 0)),
                  _window_spec(n_a, k, lambda i: (li, row_a, 0)),
                  _window_spec(n_b, k, lambda i: (li, row_b, 0))],
        out_specs=pl.BlockSpec((tm, LANE), lambda i: (i, 0)),
        out_shape=jax.ShapeDtypeStruct((m, LANE), F32),
        compiler_params=_cparams(("parallel",)),
        name="gate_proj",
    )(x, wt, wt)


def _mm_res_kernel(x_ref, w_ref, r_ref, o_ref):
    k = pl.program_id(2)

    @pl.when(k == 0)
    def _():
        o_ref[...] = r_ref[...]

    o_ref[...] += _dot(x_ref[...], w_ref[...].astype(BF16))


def matmul_residual(x, w, li, res, tm, tn, tk):
    m, k = x.shape
    n = w.shape[2]
    return pl.pallas_call(
        _mm_res_kernel,
        grid=(m // tm, n // tn, k // tk),
        in_specs=[_act_spec((tm, tk), lambda i, j, kk: (i, kk)) if k == tk
                  else pl.BlockSpec((tm, tk), lambda i, j, kk: (i, kk)),
                  pl.BlockSpec((None, tk, tn), lambda i, j, kk: (li, kk, j)),
                  pl.BlockSpec((tm, tn), lambda i, j, kk: (i, j))],
        out_specs=pl.BlockSpec((tm, tn), lambda i, j, kk: (i, j)),
        out_shape=jax.ShapeDtypeStruct((m, n), F32),
        compiler_params=_cparams(("parallel", "arbitrary", "arbitrary")),
        name="matmul_residual",
    )(x, w, res)


def _ple_kernel(x_ref, wg_ref, p_ref, wp_ref, r_ref, o_ref):
    gate = jax.nn.sigmoid(_dot(x_ref[...], wg_ref[...].astype(BF16)))
    o_ref[...] = r_ref[...] + gate * _dot(p_ref[...], wp_ref[...].astype(BF16))


def ple_update(xn, wg, pe, wp, li, res, tm, tn):
    m, k = xn.shape
    n = wg.shape[2]
    kp = pe.shape[1]
    return pl.pallas_call(
        _ple_kernel,
        grid=(m // tm, n // tn),
        in_specs=[_act_spec((tm, k), lambda i, j: (i, 0)),
                  pl.BlockSpec((None, k, tn), lambda i, j: (li, 0, j)),
                  pl.BlockSpec((tm, kp), lambda i, j: (i, 0)),
                  pl.BlockSpec((None, kp, tn), lambda i, j: (li, 0, j)),
                  pl.BlockSpec((tm, tn), lambda i, j: (i, j))],
        out_specs=pl.BlockSpec((tm, tn), lambda i, j: (i, j)),
        out_shape=jax.ShapeDtypeStruct((m, n), F32),
        compiler_params=_cparams(("parallel", "arbitrary")),
        name="ple_update",
    )(xn, wg, pe, wp, res)


def _ffn_act_kernel(g_ref, v_ref, wg_ref, wv_ref, cg0_ref, cv0_ref,
                    h_ref, cg_ref, cv_ref, sg, sv, *, tr, n_row_tiles):
    i = pl.program_id(2)
    last = FFN_CONV - 1

    def conv(x_ref, w_ref, c0_ref, c_ref, scr):
        @pl.when(i == 0)
        def _():
            scr[8 - last:8, :] = c0_ref[0]

        scr[8:8 + tr, :] = x_ref[...]
        y = scr[8:8 + tr, :] * w_ref[last:last + 1, :]
        for j in range(last):
            y = y + scr[8 - last + j:8 - last + j + tr, :] * w_ref[j:j + 1, :]
        tail = scr[8 + tr - last:8 + tr, :]
        scr[8 - last:8, :] = tail

        @pl.when(i == n_row_tiles - 1)
        def _():
            c_ref[0] = tail

        return y

    gate = conv(g_ref, wg_ref, cg0_ref, cg_ref, sg)
    val = conv(v_ref, wv_ref, cv0_ref, cv_ref, sv)
    h_ref[...] = (_silu(gate) * val).astype(h_ref.dtype)


def ffn_conv_act(up, w_conv, cache, b, t, tr, tc):
    m, f2 = up.shape
    f = f2 // 2
    nj = f // tc
    n_row_tiles = t // tr
    last = FFN_CONV - 1
    kern = functools.partial(_ffn_act_kernel, tr=tr, n_row_tiles=n_row_tiles)
    return pl.pallas_call(
        kern,
        grid=(b, nj, n_row_tiles),
        in_specs=[pl.BlockSpec((tr, tc), lambda bb, j, i: (bb * n_row_tiles + i, j)),
                  pl.BlockSpec((tr, tc), lambda bb, j, i: (bb * n_row_tiles + i, nj + j)),
                  pl.BlockSpec((FFN_CONV, tc), lambda bb, j, i: (0, j)),
                  pl.BlockSpec((FFN_CONV, tc), lambda bb, j, i: (0, nj + j)),
                  pl.BlockSpec((1, last, tc), lambda bb, j, i: (bb, 0, j)),
                  pl.BlockSpec((1, last, tc), lambda bb, j, i: (bb, 0, nj + j))],
        out_specs=[pl.BlockSpec((tr, tc), lambda bb, j, i: (bb * n_row_tiles + i, j)),
                   pl.BlockSpec((1, last, tc), lambda bb, j, i: (bb, 0, j)),
                   pl.BlockSpec((1, last, tc), lambda bb, j, i: (bb, 0, j))],
        out_shape=[jax.ShapeDtypeStruct((m, f), BF16),
                   jax.ShapeDtypeStruct((b, last, f), F32),
                   jax.ShapeDtypeStruct((b, last, f), F32)],
        scratch_shapes=[pltpu.VMEM((8 + tr, tc), F32), pltpu.VMEM((8 + tr, tc), F32)],
        compiler_params=_cparams(("parallel", "parallel", "arbitrary")),
        name="ffn_conv_act",
    )(up, up, w_conv, w_conv, cache, cache)


def _ffn_up_kernel(x_ref, wg_ref, wv_ref, cwg_ref, cwv_ref, cg0_ref, cv0_ref,
                   h_ref, cg_ref, cv_ref, *, tm):
    last = FFN_CONV - 1
    x = x_ref[...]
    row = lax.broadcasted_iota(jnp.int32, (8, 1), 0)

    def conv(w_ref, cw_ref, c0_ref, c_ref):
        up = _dot(x, w_ref[...].astype(BF16))
        c_ref[0] = up[tm - last:tm, :]
        c0 = c0_ref[0]
        y = up * cw_ref[last:last + 1, :]
        for s in range(1, last + 1):
            sh = pltpu.roll(up, s, axis=0)
            head = sh[0:8]
            for r in range(s):
                head = jnp.where(row == r, c0[last - s + r:last - s + r + 1, :], head)
            y = y + jnp.concatenate([head, sh[8:]], axis=0) * cw_ref[last - s:last - s + 1, :]
        return y

    gate = conv(wg_ref, cwg_ref, cg0_ref, cg_ref)
    val = conv(wv_ref, cwv_ref, cv0_ref, cv_ref)
    h_ref[...] = (_silu(gate) * val).astype(h_ref.dtype)


def ffn_up_fused(xn, w_up, li, w_conv, cache, tn):
    m, d = xn.shape
    b = cache.shape[0]
    tm = m // b
    f = w_up.shape[2] // 2
    nj = f // tn
    last = FFN_CONV - 1
    kern = functools.partial(_ffn_up_kernel, tm=tm)
    cspec = lambda off: pl.BlockSpec((1, last, tn), lambda i, j: (i, 0, off + j))
    return pl.pallas_call(
        kern,
        grid=(b, nj),
        in_specs=[_act_spec((tm, d), lambda i, j: (i, 0), tight=True),
                  pl.BlockSpec((None, d, tn), lambda i, j: (li, 0, j)),
                  pl.BlockSpec((None, d, tn), lambda i, j: (li, 0, nj + j)),
                  pl.BlockSpec((FFN_CONV, tn), lambda i, j: (0, j)),
                  pl.BlockSpec((FFN_CONV, tn), lambda i, j: (0, nj + j)),
                  cspec(0), cspec(nj)],
        out_specs=[pl.BlockSpec((tm, tn), lambda i, j: (i, j)), cspec(0), cspec(0)],
        out_shape=[jax.ShapeDtypeStruct((m, f), BF16),
                   jax.ShapeDtypeStruct((b, last, f), F32),
                   jax.ShapeDtypeStruct((b, last, f), F32)],
        compiler_params=_cparams(("parallel", "arbitrary")),
        name="ffn_up_fused",
    )(xn, w_up, w_up, w_conv, w_conv, cache, cache)


def _levels(c):
    out, h = [], c // 2
    while h >= 1:
        out.append(h)
        h //= 2
    return out


def _prefix_matrix(c):
    blocks = [np.tril(np.ones((c, c), np.float32))]
    for h in _levels(c):
        p = np.zeros((c, c), np.float32)
        for t in range(c):
            mid = (t // (2 * h)) * 2 * h + h
            if t % (2 * h) >= h:
                p[t, mid:t + 1] = 1.0
            else:
                p[t, t + 1:mid] = 1.0
        blocks.append(p)
    return np.concatenate(blocks, axis=0)


def _gated_head_out(o, gate, w):
    o = o * lax.rsqrt(jnp.mean(o * o, axis=-1, keepdims=True) + EPS) * w
    return o * _silu(gate)


def _stack_heads(x, nh, w):
    return jnp.concatenate([x[:, i * w:(i + 1) * w] for i in range(nh)], axis=0)


def _diag_blocks(y, nh, c, w):
    return jnp.concatenate([y[i * c:(i + 1) * c, i * w:(i + 1) * w] for i in range(nh)], axis=0)


def _block_cols(v, nh, c):
    head = lax.broadcasted_iota(jnp.int32, (nh * c, 1), 0) >> (c.bit_length() - 1)
    return jnp.concatenate([jnp.where(head == i, v, 0.0) for i in range(nh)], axis=-1)


def _row_to_col(row):
    n = row.shape[1]
    eye = (lax.broadcasted_iota(jnp.int32, (n, n), 0) == lax.broadcasted_iota(jnp.int32, (n, n), 1))
    return jnp.sum(jnp.where(eye, row, 0.0), axis=1, keepdims=True)


def _each(f, *lists):
    return [f(*args) for args in zip(*lists)]


def _linear_attn_blocks(q, k, v, g, s_cat, p_ref, c, nh, dv):
    dk = q[0].shape[1] // nh
    r = nh * c
    z = [_sel_dot(p_ref[...], g_) for g_ in g]
    b = [z_[0:c] for z_ in z]
    row = lax.broadcasted_iota(jnp.int32, (c, 1), 0)
    ri = lax.broadcasted_iota(jnp.int32, (r, r), 0)
    ci = lax.broadcasted_iota(jnp.int32, (r, r), 1)
    attn = _each(lambda q_, k_: jnp.where(
        ri == ci, jnp.sum(_stack_heads(q_ * k_, nh, dk), axis=-1, keepdims=True), 0.0), q, k)
    for li, h in enumerate(_levels(c)):
        lower = (row & (2 * h - 1)) >= h
        same = (ri >> ((2 * h).bit_length() - 1)) == (ci >> ((2 * h).bit_length() - 1))

        def level(q_, k_, z_):
            e = jnp.exp(z_[(li + 1) * c:(li + 2) * c])
            ql = _stack_heads(jnp.where(lower, q_ * e, 0.0), nh, dk).astype(BF16)
            kl = _stack_heads(jnp.where(lower, 0.0, k_ * e), nh, dk).astype(BF16)
            return jnp.where(same, _dot_nt(ql, kl), 0.0)

        attn = _each(lambda a, q_, k_, z_: a + level(q_, k_, z_), attn, q, k, z)
    vs = [_stack_heads(v_, nh, dv) for v_ in v]
    o = _each(lambda q_, b_, s_, a, vs_: _diag_blocks(
        _mm(_stack_heads(q_ * jnp.exp(b_), nh, dk), s_), nh, c, dv) + _mm(a, vs_), q, b, s_cat, attn, vs)

    def new_state(k_, b_, vs_, s_):
        b_last = b_[c - 1:c]
        kt = _stack_heads(k_ * jnp.exp(b_last - b_), nh, dk).astype(BF16)
        upd = _dot_tn(kt, _block_cols(vs_, nh, c).astype(BF16))
        el = jnp.exp(b_last)
        dec = jnp.concatenate(
            [jnp.broadcast_to(_row_to_col(el[:, i * dk:(i + 1) * dk]), (dk, dv)) for i in range(nh)],
            axis=-1)
        return dec * s_ + upd

    return o, _each(new_state, k, b, vs, s_cat)


def _gla_kernel(q_ref, k_ref, v_ref, gg_ref, sm_ref, wg_ref, bg_ref, nw_ref, p_ref, s0_ref,
                o_ref, s_ref, *, c):
    ci = pl.program_id(1)

    @pl.when(ci == 0)
    def _():
        s_ref[...] = s0_ref[...]

    x = _dot_sel(sm_ref[...], wg_ref[...]) + bg_ref[...]
    g = _log_sigmoid(x) * (1.0 / GLA_GATE_NORM)
    q = q_ref[...] * (GLA_DK ** -0.5)
    gate = gg_ref[...]
    s_cat = jnp.concatenate([s_ref[0, hh] for hh in range(GLA_H)], axis=-1)
    (o,), (s_new,) = _linear_attn_blocks([q], [k_ref[...]], [v_ref[...]], [g], [s_cat], p_ref, c,
                                         GLA_H, GLA_DV)
    for hh in range(GLA_H):
        vs = slice(hh * GLA_DV, (hh + 1) * GLA_DV)
        s_ref[0, hh] = s_new[:, vs]
        o_ref[:, vs] = _gated_head_out(o[hh * c:(hh + 1) * c], gate[:, vs],
                                       nw_ref[...]).astype(o_ref.dtype)


def gla_mixer(proj, small, wg_pad, bg, norm_w, state, b, t, c):
    m = proj.shape[0]
    nc = t // c
    kw = GLA_H * GLA_DK
    vw = GLA_H * GLA_DV
    p = jnp.asarray(_prefix_matrix(c), BF16)
    row = lambda bb, cc: bb * nc + cc
    kern = functools.partial(_gla_kernel, c=c)
    return pl.pallas_call(
        kern,
        grid=(b, nc),
        in_specs=[pl.BlockSpec((c, kw), lambda bb, cc: (row(bb, cc), 0)),
                  pl.BlockSpec((c, kw), lambda bb, cc: (row(bb, cc), 1)),
                  pl.BlockSpec((c, vw), lambda bb, cc: (row(bb, cc), 1)),
                  pl.BlockSpec((c, vw), lambda bb, cc: (row(bb, cc), 2)),
                  pl.BlockSpec((c, LANE), lambda bb, cc: (row(bb, cc), 0)),
                  pl.BlockSpec((LANE, kw), lambda bb, cc: (0, 0)),
                  pl.BlockSpec((1, kw), lambda bb, cc: (0, 0)),
                  pl.BlockSpec((1, GLA_DV), lambda bb, cc: (0, 0)),
                  pl.BlockSpec(p.shape, lambda bb, cc: (0, 0)),
                  pl.BlockSpec((1, GLA_H, GLA_DK, GLA_DV), lambda bb, cc: (bb, 0, 0, 0))],
        out_specs=[pl.BlockSpec((c, vw), lambda bb, cc: (row(bb, cc), 0)),
                   pl.BlockSpec((1, GLA_H, GLA_DK, GLA_DV), lambda bb, cc: (bb, 0, 0, 0))],
        out_shape=[jax.ShapeDtypeStruct((m, vw), BF16),
                   jax.ShapeDtypeStruct((b, GLA_H, GLA_DK, GLA_DV), F32)],
        compiler_params=_cparams(("parallel", "arbitrary")),
        name="gla_mixer",
    )(proj, proj, proj, proj, small, wg_pad, bg.reshape(1, -1), norm_w.reshape(1, -1), p, state)


HG_HB = 4
HG_GS = 2


def _hgrn_kernel(q_ref, f_ref, i_ref, g_ref, lb_ref, nw_ref, p_ref, s0_ref,
                 o_ref, s_ref, *, c):
    ci = pl.program_id(2)

    @pl.when(ci == 0)
    def _():
        s_ref[...] = s0_ref[...]

    lb = lb_ref[...]
    log_lb = jnp.log(jnp.maximum(lb, TINY))
    zf = f_ref[...]
    a = log_lb
    cc = jnp.log1p(-lb) + _log_sigmoid(zf)
    log_f = jnp.maximum(a, cc) + jnp.log1p(jnp.exp(-jnp.abs(a - cc)))
    key = (1.0 - lb) * jax.nn.sigmoid(-zf)
    q = _silu(q_ref[...])
    v = i_ref[...]
    gate = g_ref[...]
    gw = HG_HB * HG_DK
    groups = [slice(gi * gw, (gi + 1) * gw) for gi in range(HG_GS)]
    s_cat = [jnp.concatenate([s_ref[0, gi * HG_HB + hh] for hh in range(HG_HB)], axis=-1)
             for gi in range(HG_GS)]
    o, s_new = _linear_attn_blocks([q[:, gs] for gs in groups], [key[:, gs] for gs in groups],
                                   [v[:, gs] for gs in groups], [log_f[:, gs] for gs in groups],
                                   s_cat, p_ref, c, HG_HB, HG_DV)
    for gi in range(HG_GS):
        for hh in range(HG_HB):
            sl = slice(hh * HG_DV, (hh + 1) * HG_DV)
            ol = slice(gi * gw + hh * HG_DV, gi * gw + (hh + 1) * HG_DV)
            s_ref[0, gi * HG_HB + hh] = s_new[gi][:, sl]
            o_ref[:, ol] = _gated_head_out(o[gi][hh * c:(hh + 1) * c], gate[:, ol],
                                           nw_ref[...]).astype(o_ref.dtype)


def hgrn_mixer(proj, lb, norm_w, state, b, t, c):
    col0 = 0
    m = proj.shape[0]
    nc = t // c
    hs = HG_HB * HG_GS
    ng = HG_H // hs
    w = hs * HG_DK
    p = jnp.asarray(_prefix_matrix(c), BF16)
    row = lambda bb, g, cc: bb * nc + cc
    kern = functools.partial(_hgrn_kernel, c=c)
    return pl.pallas_call(
        kern,
        grid=(b, ng, nc),
        in_specs=[pl.BlockSpec((c, w), lambda bb, g, cc: (row(bb, g, cc), col0 + g)),
                  pl.BlockSpec((c, w), lambda bb, g, cc: (row(bb, g, cc), col0 + ng + g)),
                  pl.BlockSpec((c, w), lambda bb, g, cc: (row(bb, g, cc), col0 + 2 * ng + g)),
                  pl.BlockSpec((c, w), lambda bb, g, cc: (row(bb, g, cc), col0 + 3 * ng + g)),
                  pl.BlockSpec((1, w), lambda bb, g, cc: (0, g)),
                  pl.BlockSpec((1, HG_DV), lambda bb, g, cc: (0, 0)),
                  pl.BlockSpec(p.shape, lambda bb, g, cc: (0, 0)),
                  pl.BlockSpec((1, hs, HG_DK, HG_DV), lambda bb, g, cc: (bb, g, 0, 0))],
        out_specs=[pl.BlockSpec((c, w), lambda bb, g, cc: (row(bb, g, cc), g)),
                   pl.BlockSpec((1, hs, HG_DK, HG_DV), lambda bb, g, cc: (bb, g, 0, 0))],
        out_shape=[jax.ShapeDtypeStruct((m, HG_H * HG_DV), BF16),
                   jax.ShapeDtypeStruct((b, HG_H, HG_DK, HG_DV), F32)],
        compiler_params=_cparams(("parallel", "parallel", "arbitrary")),
        name="hgrn_mixer",
    )(proj, proj, proj, proj, lb.reshape(1, -1), norm_w.reshape(1, -1), p, state)


GDN_HB = 4
GDN_GS = 4


def _unit_lower_inverse(ns, r, c):
    ri = lax.broadcasted_iota(jnp.int32, (r, r), 0)
    ci = lax.broadcasted_iota(jnp.int32, (r, r), 1)
    eye = (ri == ci).astype(F32)
    d = [jnp.where((ri >> 4) == (ci >> 4), n, 0.0) for n in ns]
    x = [eye - d_ for d_ in d]
    p = d
    for _ in range(3):
        p = _each(_mm, p, p)
        x = _each(lambda x_, p_: x_ + _mm(x_, p_), x, p)
    size = 16
    while size < c:
        sh = size.bit_length() - 1
        mask = ((ri >> (sh + 1)) == (ci >> (sh + 1))) & ((ri >> sh) > (ci >> sh))
        lx = _each(lambda n, x_: _mm(jnp.where(mask, n, 0.0), x_), ns, x)
        x = _each(lambda x_, lx_: x_ - _mm(x_, lx_), x, lx)
        size *= 2
    return x


def _gdn_blocks(q, k, v, beta, b_c, b_last, s_cat, c, nh):
    r = nh * c
    ri = lax.broadcasted_iota(jnp.int32, (r, r), 0)
    ci = lax.broadcasted_iota(jnp.int32, (r, r), 1)
    shc = c.bit_length() - 1
    causal = ((ri >> shc) == (ci >> shc)) & (ri >= ci)

    def decay_of(bc):
        b_r = jnp.sum(jnp.where(ri == ci, bc, 0.0), axis=0, keepdims=True)
        return jnp.where(causal, jnp.exp(jnp.minimum(bc - b_r, 0.0)), 0.0)

    decay = _each(decay_of, b_c)
    kb = [k_.astype(BF16) for k_ in k]
    kq = _each(lambda kb_, q_: _dot_nt(jnp.concatenate([kb_, q_.astype(BF16)], axis=0), kb_), kb, q)
    n = _each(lambda be, kq_, de: jnp.where(ri > ci, be * kq_[:r] * de, 0.0), beta, kq, decay)
    tinv = _unit_lower_inverse(n, r, c)
    eb = [jnp.exp(bc) for bc in b_c]
    sol = _each(lambda t_, v_, k_, be, e_: _mm(t_, jnp.concatenate([v_ * be, k_ * (be * e_)], axis=-1)),
                tinv, v, k, beta, eb)
    ws_qs = _each(lambda so, q_, e_, s_: _mm(jnp.concatenate([so[:, GDN_DV:], q_ * e_], axis=0), s_),
                  sol, q, eb, s_cat)
    v_new = _each(lambda so, wq: so[:, :GDN_DV] - _diag_blocks(wq[:r], nh, c, GDN_DV), sol, ws_qs)
    o = _each(lambda wq, kq_, de, vn: _diag_blocks(wq[r:], nh, c, GDN_DV) + _mm(kq_[r:] * de, vn),
              ws_qs, kq, decay, v_new)
    upd = _each(lambda k_, bl, bc, vn: _dot_tn((k_ * jnp.exp(bl - bc)).astype(BF16),
                                               _block_cols(vn, nh, c).astype(BF16)),
                k, b_last, b_c, v_new)

    def decayed(bl, s_, up):
        el = jnp.exp(bl)
        dec = jnp.concatenate(
            [jnp.broadcast_to(el[i * c:i * c + 1], (1, GDN_DV)) for i in range(nh)], axis=-1)
        return dec * s_ + up

    return o, _each(decayed, b_last, s_cat, upd)


def _gdn_kernel(q_ref, k_ref, v_ref, z_ref, gc_ref, wq_ref, wk_ref, wv_ref,
                pc_ref, nw_ref, tril_ref, s0_ref, cq0_ref, ck0_ref, cv0_ref,
                o_ref, s_ref, cq_ref, ck_ref, cv_ref,
                xq, xk, xv, *, c, nc):
    ci = pl.program_id(2)
    last = GDN_CONV - 1

    @pl.when(ci == 0)
    def _():
        s_ref[...] = s0_ref[...]

    def conv(x_ref, w_ref, c0_ref, c_ref, scr):
        @pl.when(ci == 0)
        def _():
            scr[8 - last:8, :] = c0_ref[0]

        scr[8:8 + c, :] = x_ref[...]
        y = scr[8:8 + c, :] * w_ref[last:last + 1, :]
        for j in range(last):
            y = y + scr[8 - last + j:8 - last + j + c, :] * w_ref[j:j + 1, :]
        tail = scr[8 + c - last:8 + c, :]
        scr[8 - last:8, :] = tail

        @pl.when(ci == nc - 1)
        def _():
            c_ref[0] = tail

        return _silu(y)

    qa = conv(q_ref, wq_ref, cq0_ref, cq_ref, xq)
    ka = conv(k_ref, wk_ref, ck0_ref, ck_ref, xk)
    va = conv(v_ref, wv_ref, cv0_ref, cv_ref, xv)

    nh = GDN_HB
    gw = nh * GDN_DK
    q, k, v, beta, b_c, b_last, s_cat = [], [], [], [], [], [], []
    for gi in range(GDN_GS):
        gcol = gc_ref[gi]
        pc = pc_ref[gi]
        beta_all = jax.nn.sigmoid(gcol)
        g_c = pc[0:1, :] * _softplus(gcol + pc[1:2, :])
        b_col = _sel_dot(tril_ref[...], g_c)
        beta.append(jnp.concatenate([beta_all[:, hh:hh + 1] for hh in range(nh)], axis=0))
        b_c.append(jnp.concatenate([b_col[:, nh + hh:nh + hh + 1] for hh in range(nh)], axis=0))
        b_last.append(jnp.concatenate(
            [jnp.broadcast_to(b_col[c - 1:c, nh + hh:nh + hh + 1], (c, 1)) for hh in range(nh)],
            axis=0))
        gs = slice(gi * gw, (gi + 1) * gw)
        qg = _stack_heads(qa[:, gs], nh, GDN_DK)
        kg = _stack_heads(ka[:, gs], nh, GDN_DK)
        q.append(qg * lax.rsqrt(jnp.sum(qg * qg, axis=-1, keepdims=True) + EPS) * (GDN_DK ** -0.5))
        k.append(kg * lax.rsqrt(jnp.sum(kg * kg, axis=-1, keepdims=True) + EPS))
        v.append(_stack_heads(va[:, gs], nh, GDN_DV))
        s_cat.append(jnp.concatenate([s_ref[0, gi * nh + hh] for hh in range(nh)], axis=-1))
    o, s_new = _gdn_blocks(q, k, v, beta, b_c, b_last, s_cat, c, nh)
    z = z_ref[...]
    for gi in range(GDN_GS):
        for hh in range(nh):
            sl = slice(hh * GDN_DV, (hh + 1) * GDN_DV)
            zl = slice(gi * gw + hh * GDN_DV, gi * gw + (hh + 1) * GDN_DV)
            s_ref[0, gi * nh + hh] = s_new[gi][:, sl]
            o_ref[:, zl] = _gated_head_out(o[gi][hh * c:(hh + 1) * c], z[:, zl],
                                           nw_ref[...]).astype(o_ref.dtype)


def gdn_mixer(proj, gates_c, conv_w, par_c, norm_w, state, cache, b, t, c):
    m = proj.shape[0]
    nc = t // c
    hs = GDN_HB * GDN_GS
    ng = GDN_H // hs
    w = hs * GDN_DK
    col0 = 0
    last = GDN_CONV - 1
    tril = jnp.asarray(np.tril(np.ones((c, c), np.float32)), BF16)
    row = lambda bb, g, cc: bb * nc + cc
    kern = functools.partial(_gdn_kernel, c=c, nc=nc)
    big = lambda off: pl.BlockSpec((c, w), lambda bb, g, cc: (row(bb, g, cc), col0 + off * ng + g))
    cw = lambda off: pl.BlockSpec((GDN_CONV, w), lambda bb, g, cc: (0, off * ng + g))
    c0 = lambda off: pl.BlockSpec((1, last, w), lambda bb, g, cc: (bb, 0, off * ng + g))
    cout = pl.BlockSpec((1, last, w), lambda bb, g, cc: (bb, 0, g))
    outs = pl.pallas_call(
        kern,
        grid=(b, ng, nc),
        in_specs=[big(0), big(1), big(2), big(3),
                  pl.BlockSpec((GDN_GS, c, LANE), lambda bb, g, cc: (g, row(bb, g, cc), 0)),
                  cw(0), cw(1), cw(2),
                  pl.BlockSpec((GDN_GS, 2, LANE), lambda bb, g, cc: (g, 0, 0)),
                  pl.BlockSpec((1, GDN_DV), lambda bb, g, cc: (0, 0)),
                  pl.BlockSpec((c, c), lambda bb, g, cc: (0, 0)),
                  pl.BlockSpec((1, hs, GDN_DK, GDN_DV), lambda bb, g, cc: (bb, g, 0, 0)),
                  c0(0), c0(1), c0(2)],
        out_specs=[pl.BlockSpec((c, w), lambda bb, g, cc: (row(bb, g, cc), g)),
                   pl.BlockSpec((1, hs, GDN_DK, GDN_DV), lambda bb, g, cc: (bb, g, 0, 0)),
                   cout, cout, cout],
        out_shape=[jax.ShapeDtypeStruct((m, GDN_H * GDN_DV), BF16),
                   jax.ShapeDtypeStruct((b, GDN_H, GDN_DK, GDN_DV), F32),
                   jax.ShapeDtypeStruct((b, last, GDN_H * GDN_DK), F32),
                   jax.ShapeDtypeStruct((b, last, GDN_H * GDN_DK), F32),
                   jax.ShapeDtypeStruct((b, last, GDN_H * GDN_DV), F32)],
        scratch_shapes=[pltpu.VMEM((8 + c, w), F32), pltpu.VMEM((8 + c, w), F32),
                        pltpu.VMEM((8 + c, w), F32)],
        compiler_params=_cparams(("parallel", "parallel", "arbitrary")),
        name="gdn_mixer",
    )(proj, proj, proj, proj, gates_c, conv_w, conv_w, conv_w, par_c,
      norm_w.reshape(1, -1), tril, state, cache, cache, cache)
    o, s, cq, ck, cv = outs
    return o, s, jnp.concatenate([cq, ck, cv], axis=-1)


def _row_tile(m, cap):
    tm = min(m, cap)
    while m % tm:
        tm //= 2
    return tm


def _group_gates(small):
    m = small.shape[0]
    ng = GDN_H // GDN_HB
    db = small[:, GLA_RANK:GLA_RANK + GDN_H].reshape(m, ng, GDN_HB)
    da = small[:, GLA_RANK + GDN_H:GLA_RANK + 2 * GDN_H].reshape(m, ng, GDN_HB)
    g = jnp.concatenate([db, da, jnp.zeros((m, ng, LANE - 2 * GDN_HB), F32)], axis=-1)
    return jnp.transpose(g, (1, 0, 2))


def _group_params(a_log, dt_bias):
    ng = GDN_H // GDN_HB
    na = -jnp.exp(a_log.astype(F32)).reshape(ng, GDN_HB)
    dt = dt_bias.astype(F32).reshape(ng, GDN_HB)
    pad = jnp.zeros((ng, GDN_HB), F32)
    tail = jnp.zeros((ng, LANE - 2 * GDN_HB), F32)
    return jnp.stack([jnp.concatenate([pad, na, tail], axis=-1),
                      jnp.concatenate([pad, dt, tail], axis=-1)], axis=1)


COL_GLA = 0
N_GLA = 2 * GLA_H * GLA_DK + 2 * GLA_H * GLA_DV
COL_LR = COL_GLA + N_GLA
COL_GDN = COL_LR + GLA_RANK
N_GDN = 2 * GDN_H * GDN_DK + 2 * GDN_H * GDN_DV
COL_BA = COL_GDN + N_GDN
COL_HG = COL_BA + 2 * GDN_H
N_HG = 2 * HG_H * HG_DK + 2 * HG_H * HG_DV

COL_TILE = 256
DOWN_TILES = (1024, 512)


def _prep_weights(w_gla_gate, w_down):
    wg_pad = jnp.pad(w_gla_gate.astype(BF16), ((0, 0), (0, LANE - GLA_RANK), (0, 0)))
    return wg_pad, w_down.astype(BF16)


def _trunk(x, pe, st_gla, st_gdn, cb_gdn, st_hg, cb_ffn, wts, prm, lower_bounds, norm_final):
    (w_in_t, wg_pad, w_out, w_up, w_down, w_pg, w_pp) = wts
    (norm_mix, b_gla_gate, gla_norm, w_gdn_conv, gdn_a_log, gdn_dt_bias, gdn_norm, hgrn_norm,
     norm_ffn, w_ffn_conv, norm_ple) = prm
    b, t, d = x.shape
    depth = w_in_t.shape[0]
    m = b * t
    c = min(MAX_CHUNK, t)
    tm = _row_tile(m, ROW_TILE)
    tr = _row_tile(t, 256)
    tn = COL_TILE
    f = w_down.shape[1]
    h = x.reshape(m, d)
    n_gla, n_gdn, n_gconv, n_hg, n_fconv = [], [], [], [], []
    for li in range(depth):
        xn = rmsnorm(h, norm_mix[li], BF16, _row_tile(m, 256))
        p_gla = matmul_t(xn, w_in_t, li, F32, tm, tn, COL_GLA, N_GLA)
        p_gdn = matmul_t(xn, w_in_t, li, F32, tm, tn, COL_GDN, N_GDN)
        p_hg = matmul_t(xn, w_in_t, li, F32, tm, tn, COL_HG, N_HG)
        small = gate_proj(xn, w_in_t, li, _row_tile(m, 1024), COL_LR, GLA_RANK, COL_BA, 2 * GDN_H)

        o_gla, s_gla = gla_mixer(p_gla, small, wg_pad[li], b_gla_gate[li], gla_norm[li],
                                 st_gla[li], b, t, c)
        o_gdn, s_gdn, b_gdn = gdn_mixer(p_gdn, _group_gates(small), w_gdn_conv[li],
                                        _group_params(gdn_a_log[li], gdn_dt_bias[li]), gdn_norm[li],
                                        st_gdn[li], cb_gdn[li], b, t, c)
        o_hg, s_hg = hgrn_mixer(p_hg, lower_bounds[li], hgrn_norm[li], st_hg[li], b, t, c)
        mix = jnp.concatenate([o_gla, o_gdn, o_hg], axis=-1)
        h = matmul_residual(mix, w_out, li, h, tm, tn, mix.shape[1])

        xn = rmsnorm(h, norm_ffn[li], BF16, _row_tile(m, 256))
        if t == tm:
            hid, cg, cv = ffn_up_fused(xn, w_up, li, w_ffn_conv[li], cb_ffn[li], tn)
        else:
            up = matmul(xn, w_up, li, F32, tm, tn)
            hid, cg, cv = ffn_conv_act(up, w_ffn_conv[li], cb_ffn[li], b, t, tr, f // 2)
        b_ffn = jnp.concatenate([cg, cv], axis=-1)
        h = matmul_residual(hid, w_down, li, h, _row_tile(m, DOWN_TILES[0]), DOWN_TILES[1], f // 2)

        xn = rmsnorm(h, norm_ple[li], BF16, _row_tile(m, 256))
        h = ple_update(xn, w_pg, pe[li].reshape(m, -1).astype(BF16), w_pp, li, h, tm, tn)

        n_gla.append(s_gla)
        n_gdn.append(s_gdn)
        n_gconv.append(b_gdn)
        n_hg.append(s_hg)
        n_fconv.append(b_ffn)
    y = rmsnorm(h, norm_final, F32, _row_tile(m, 256)).reshape(b, t, d)
    return (y, jnp.stack(n_gla), jnp.stack(n_gdn), jnp.stack(n_gconv),
            jnp.stack(n_hg), jnp.stack(n_fconv))


def kernel(x_prompt, x_sample, p_prompt, p_sample, state_gla, state_gdn, cache_gdn_conv, state_hgrn, cache_ffn_conv, norm_mix, w_in, w_gla_gate, b_gla_gate, gla_norm, w_gdn_conv, gdn_a_log, gdn_dt_bias, gdn_norm, hgrn_lb, hgrn_norm, w_out, norm_ffn, w_up, w_ffn_conv, w_down, norm_ple, w_ple_gate, w_ple_proj, norm_final):
    depth = w_in.shape[0]
    bp = x_prompt.shape[0]
    wg_pad, w_down_b = _prep_weights(w_gla_gate, w_down)
    wts = (jnp.swapaxes(w_in, 1, 2), wg_pad, w_out, w_up, w_down_b, w_ple_gate, w_ple_proj)
    prm = (norm_mix, b_gla_gate, gla_norm, w_gdn_conv, gdn_a_log, gdn_dt_bias, gdn_norm,
           hgrn_norm, norm_ffn, w_ffn_conv, norm_ple)
    sm = jax.nn.softmax(hgrn_lb.astype(F32), axis=0)
    lower_bounds = jnp.cumsum(sm, axis=0) - sm[0]
    zeros = lambda a: jnp.zeros((depth, bp) + a.shape[2:], F32)
    prompt = _trunk(x_prompt, p_prompt, zeros(state_gla), zeros(state_gdn), zeros(cache_gdn_conv),
                    zeros(state_hgrn), zeros(cache_ffn_conv), wts, prm, lower_bounds, norm_final)
    sample = _trunk(x_sample, p_sample, state_gla, state_gdn, cache_gdn_conv, state_hgrn,
                    cache_ffn_conv, wts, prm, lower_bounds, norm_final)
    return (prompt[0], sample[0]) + prompt[1:] + sample[1:]
```

```python
import functools

import numpy as np
import jax
import jax.numpy as jnp
from jax import lax
from jax.experimental import pallas as pl
from jax.experimental.pallas import tpu as pltpu

F32 = jnp.float32
BF16 = jnp.bfloat16

EPS = 1e-6
TINY = 1e-30
GLA_GATE_NORM = 16.0
GLA_H, GLA_DK, GLA_DV, GLA_RANK = 4, 128, 256, 16
GDN_H, GDN_DK, GDN_DV, GDN_CONV = 16, 128, 128, 4
HG_H, HG_DK, HG_DV = 8, 128, 128
FFN_CONV = 3
MAX_CHUNK = 64
ROW_TILE = 2048

LANE = 128
VMEM_LIMIT = 60 * 1024 * 1024


def _cparams(sem):
    return pltpu.CompilerParams(dimension_semantics=sem, vmem_limit_bytes=VMEM_LIMIT)


def _dot(a, b):
    return jnp.dot(a, b, preferred_element_type=F32)


def _dot_nt(a, b):
    return lax.dot_general(a, b, (((1,), (1,)), ((), ())), preferred_element_type=F32)


def _dot_tn(a, b):
    return lax.dot_general(a, b, (((0,), (0,)), ((), ())), preferred_element_type=F32)


def _mm(a, b):
    return _dot(a.astype(BF16), b.astype(BF16))


def _split3(x):
    hi = x.astype(BF16)
    r = x - hi.astype(F32)
    mid = r.astype(BF16)
    lo = (r - mid.astype(F32)).astype(BF16)
    return hi, mid, lo


def _sel_dot(p, x):
    hi, mid, lo = _split3(x)
    return _dot(p, hi) + _dot(p, mid) + _dot(p, lo)


def _dot_sel(x, p):
    hi, mid, lo = _split3(x)
    return _dot(hi, p) + _dot(mid, p) + _dot(lo, p)


def _softplus(x):
    return jnp.maximum(x, 0.0) + jnp.log1p(jnp.exp(-jnp.abs(x)))


def _log_sigmoid(x):
    return -_softplus(-x)


def _silu(x):
    return x * jax.nn.sigmoid(x)


def _rmsnorm_kernel(x_ref, w_ref, o_ref):
    x = x_ref[...]
    y = x * lax.rsqrt(jnp.mean(x * x, axis=-1, keepdims=True) + EPS)
    o_ref[...] = (y * w_ref[...]).astype(o_ref.dtype)


def rmsnorm(x, w, out_dtype, tm):
    m, d = x.shape
    return pl.pallas_call(
        _rmsnorm_kernel,
        grid=(m // tm,),
        in_specs=[pl.BlockSpec((tm, d), lambda i: (i, 0)),
                  pl.BlockSpec((1, d), lambda i: (0, 0))],
        out_specs=pl.BlockSpec((tm, d), lambda i: (i, 0)),
        out_shape=jax.ShapeDtypeStruct((m, d), out_dtype),
        compiler_params=_cparams(("parallel",)),
        name="rmsnorm",
    )(x, w.reshape(1, d))


SINGLE_BUFFER_BYTES = 16 * 1024 * 1024


def _act_spec(shape, index_map, tight=False, itemsize=2):
    if tight and shape[0] * shape[1] * itemsize >= SINGLE_BUFFER_BYTES:
        return pl.BlockSpec(shape, index_map, pipeline_mode=pl.Buffered(1))
    return pl.BlockSpec(shape, index_map)


def _mm_kernel(x_ref, w_ref, o_ref):
    o_ref[...] = _dot(x_ref[...], w_ref[...].astype(BF16)).astype(o_ref.dtype)


def matmul(x, w, li, out_dtype, tm, tn):
    m, k = x.shape
    n = w.shape[2]
    return pl.pallas_call(
        _mm_kernel,
        grid=(m // tm, n // tn),
        in_specs=[_act_spec((tm, k), lambda i, j: (i, 0)),
                  pl.BlockSpec((None, k, tn), lambda i, j: (li, 0, j))],
        out_specs=pl.BlockSpec((tm, tn), lambda i, j: (i, j)),
        out_shape=jax.ShapeDtypeStruct((m, n), out_dtype),
        compiler_params=_cparams(("parallel", "arbitrary")),
        name="matmul",
    )(x, w)


def _window_spec(rows, k, index_map):
    return pl.BlockSpec((pl.Element(1), pl.Element(rows), pl.Element(k)), index_map)


def _mm_t_kernel(x_ref, wt_ref, o_ref):
    o_ref[...] = _dot_nt(x_ref[...], wt_ref[0].astype(BF16)).astype(o_ref.dtype)


def matmul_t(x, wt, li, out_dtype, tm, tn, row0, nrows):
    m, k = x.shape
    assert row0 % 8 == 0 and nrows % tn == 0
    return pl.pallas_call(
        _mm_t_kernel,
        grid=(m // tm, nrows // tn),
        in_specs=[_act_spec((tm, k), lambda i, j: (i, 0)),
                  _window_spec(tn, k, lambda i, j: (li, (row0 // 8 + j * (tn // 8)) * 8, 0))],
        out_specs=pl.BlockSpec((tm, tn), lambda i, j: (i, j)),
        out_shape=jax.ShapeDtypeStruct((m, nrows), out_dtype),
        compiler_params=_cparams(("parallel", "arbitrary")),
        name="matmul_t",
    )(x, wt)


def _gate_proj_kernel(x_ref, wa_ref, wb_ref, o_ref):
    x = x_ref[...]
    a = _dot_nt(x, wa_ref[0].astype(BF16))
    b = _dot_nt(x, wb_ref[0].astype(BF16))
    pad = jnp.zeros((x.shape[0], LANE - a.shape[1] - b.shape[1]), F32)
    o_ref[...] = jnp.concatenate([a, b, pad], axis=-1)


def gate_proj(x, wt, li, tm, row_a, n_a, row_b, n_b):
    m, k = x.shape
    assert row_a % 8 == 0 and row_b % 8 == 0
    return pl.pallas_call(
        _gate_proj_kernel,
        grid=(m // tm,),
        in_specs=[pl.BlockSpec((tm, k), lambda i: (i, 0)),
                  _window_spec(n_a, k, lambda i: (li, row_a, 0)),
                  _window_spec(n_b, k, lambda i: (li, row_b, 0))],
        out_specs=pl.BlockSpec((tm, LANE), lambda i: (i, 0)),
        out_shape=jax.ShapeDtypeStruct((m, LANE), F32),
        compiler_params=_cparams(("parallel",)),
        name="gate_proj",
    )(x, wt, wt)


def _mm_res_kernel(x_ref, w_ref, r_ref, o_ref):
    k = pl.program_id(2)

    @pl.when(k == 0)
    def _():
        o_ref[...] = r_ref[...]

    o_ref[...] += _dot(x_ref[...], w_ref[...].astype(BF16))


def matmul_residual(x, w, li, res, tm, tn, tk):
    m, k = x.shape
    n = w.shape[2]
    return pl.pallas_call(
        _mm_res_kernel,
        grid=(m // tm, n // tn, k // tk),
        in_specs=[_act_spec((tm, tk), lambda i, j, kk: (i, kk)) if k == tk
                  else pl.BlockSpec((tm, tk), lambda i, j, kk: (i, kk)),
                  pl.BlockSpec((None, tk, tn), lambda i, j, kk: (li, kk, j)),
                  pl.BlockSpec((tm, tn), lambda i, j, kk: (i, j))],
        out_specs=pl.BlockSpec((tm, tn), lambda i, j, kk: (i, j)),
        out_shape=jax.ShapeDtypeStruct((m, n), F32),
        compiler_params=_cparams(("parallel", "arbitrary", "arbitrary")),
        name="matmul_residual",
    )(x, w, res)


def _ple_kernel(x_ref, wg_ref, p_ref, wp_ref, r_ref, o_ref):
    gate = jax.nn.sigmoid(_dot(x_ref[...], wg_ref[...].astype(BF16)))
    o_ref[...] = r_ref[...] + gate * _dot(p_ref[...], wp_ref[...].astype(BF16))


def ple_update(xn, wg, pe, wp, li, res, tm, tn):
    m, k = xn.shape
    n = wg.shape[2]
    kp = pe.shape[1]
    return pl.pallas_call(
        _ple_kernel,
        grid=(m // tm, n // tn),
        in_specs=[_act_spec((tm, k), lambda i, j: (i, 0)),
                  pl.BlockSpec((None, k, tn), lambda i, j: (li, 0, j)),
                  pl.BlockSpec((tm, kp), lambda i, j: (i, 0)),
                  pl.BlockSpec((None, kp, tn), lambda i, j: (li, 0, j)),
                  pl.BlockSpec((tm, tn), lambda i, j: (i, j))],
        out_specs=pl.BlockSpec((tm, tn), lambda i, j: (i, j)),
        out_shape=jax.ShapeDtypeStruct((m, n), F32),
        compiler_params=_cparams(("parallel", "arbitrary")),
        name="ple_update",
    )(xn, wg, pe, wp, res)


def _ffn_act_kernel(g_ref, v_ref, wg_ref, wv_ref, cg0_ref, cv0_ref,
                    h_ref, cg_ref, cv_ref, sg, sv, *, tr, n_row_tiles):
    i = pl.program_id(2)
    last = FFN_CONV - 1

    def conv(x_ref, w_ref, c0_ref, c_ref, scr):
        @pl.when(i == 0)
        def _():
            scr[8 - last:8, :] = c0_ref[0]

        scr[8:8 + tr, :] = x_ref[...]
        y = scr[8:8 + tr, :] * w_ref[last:last + 1, :]
        for j in range(last):
            y = y + scr[8 - last + j:8 - last + j + tr, :] * w_ref[j:j + 1, :]
        tail = scr[8 + tr - last:8 + tr, :]
        scr[8 - last:8, :] = tail

        @pl.when(i == n_row_tiles - 1)
        def _():
            c_ref[0] = tail

        return y

    gate = conv(g_ref, wg_ref, cg0_ref, cg_ref, sg)
    val = conv(v_ref, wv_ref, cv0_ref, cv_ref, sv)
    h_ref[...] = (_silu(gate) * val).astype(h_ref.dtype)


def ffn_conv_act(up, w_conv, cache, b, t, tr, tc):
    m, f2 = up.shape
    f = f2 // 2
    nj = f // tc
    n_row_tiles = t // tr
    last = FFN_CONV - 1
    kern = functools.partial(_ffn_act_kernel, tr=tr, n_row_tiles=n_row_tiles)
    return pl.pallas_call(
        kern,
        grid=(b, nj, n_row_tiles),
        in_specs=[pl.BlockSpec((tr, tc), lambda bb, j, i: (bb * n_row_tiles + i, j)),
                  pl.BlockSpec((tr, tc), lambda bb, j, i: (bb * n_row_tiles + i, nj + j)),
                  pl.BlockSpec((FFN_CONV, tc), lambda bb, j, i: (0, j)),
                  pl.BlockSpec((FFN_CONV, tc), lambda bb, j, i: (0, nj + j)),
                  pl.BlockSpec((1, last, tc), lambda bb, j, i: (bb, 0, j)),
                  pl.BlockSpec((1, last, tc), lambda bb, j, i: (bb, 0, nj + j))],
        out_specs=[pl.BlockSpec((tr, tc), lambda bb, j, i: (bb * n_row_tiles + i, j)),
                   pl.BlockSpec((1, last, tc), lambda bb, j, i: (bb, 0, j)),
                   pl.BlockSpec((1, last, tc), lambda bb, j, i: (bb, 0, j))],
        out_shape=[jax.ShapeDtypeStruct((m, f), BF16),
                   jax.ShapeDtypeStruct((b, last, f), F32),
                   jax.ShapeDtypeStruct((b, last, f), F32)],
        scratch_shapes=[pltpu.VMEM((8 + tr, tc), F32), pltpu.VMEM((8 + tr, tc), F32)],
        compiler_params=_cparams(("parallel", "parallel", "arbitrary")),
        name="ffn_conv_act",
    )(up, up, w_conv, w_conv, cache, cache)


FFN_ROW_BLOCKS = 2


def _ffn_up_kernel(x_ref, wg_ref, wv_ref, cwg_ref, cwv_ref, cg0_ref, cv0_ref,
                   h_ref, cg_ref, cv_ref, *, tm):
    last = FFN_CONV - 1
    rb = tm // FFN_ROW_BLOCKS
    row = lax.broadcasted_iota(jnp.int32, (8, 1), 0)
    wg = wg_ref[...].astype(BF16)
    wv = wv_ref[...].astype(BF16)

    def conv(up, before, cw_ref):
        y = up * cw_ref[last:last + 1, :]
        for s in range(1, last + 1):
            sh = pltpu.roll(up, s, axis=0)
            head = sh[0:8]
            for r in range(s):
                head = jnp.where(row == r, before[last - s + r:last - s + r + 1, :], head)
            y = y + jnp.concatenate([head, sh[8:]], axis=0) * cw_ref[last - s:last - s + 1, :]
        return y

    before_g, before_v = cg0_ref[0], cv0_ref[0]
    for bi in range(FFN_ROW_BLOCKS):
        x = x_ref[bi * rb:(bi + 1) * rb, :]
        up_g = _dot(x, wg)
        up_v = _dot(x, wv)
        gate = conv(up_g, before_g, cwg_ref)
        val = conv(up_v, before_v, cwv_ref)
        h_ref[bi * rb:(bi + 1) * rb, :] = (_silu(gate) * val).astype(h_ref.dtype)
        before_g, before_v = up_g[rb - last:rb, :], up_v[rb - last:rb, :]
    cg_ref[0] = before_g
    cv_ref[0] = before_v


def ffn_up_fused(xn, w_up, li, w_conv, cache, tn):
    m, d = xn.shape
    b = cache.shape[0]
    tm = m // b
    f = w_up.shape[2] // 2
    nj = f // tn
    last = FFN_CONV - 1
    kern = functools.partial(_ffn_up_kernel, tm=tm)
    cspec = lambda off: pl.BlockSpec((1, last, tn), lambda i, j: (i, 0, off + j))
    return pl.pallas_call(
        kern,
        grid=(b, nj),
        in_specs=[_act_spec((tm, d), lambda i, j: (i, 0), tight=True),
                  pl.BlockSpec((None, d, tn), lambda i, j: (li, 0, j)),
                  pl.BlockSpec((None, d, tn), lambda i, j: (li, 0, nj + j)),
                  pl.BlockSpec((FFN_CONV, tn), lambda i, j: (0, j)),
                  pl.BlockSpec((FFN_CONV, tn), lambda i, j: (0, nj + j)),
                  cspec(0), cspec(nj)],
        out_specs=[pl.BlockSpec((tm, tn), lambda i, j: (i, j)), cspec(0), cspec(0)],
        out_shape=[jax.ShapeDtypeStruct((m, f), BF16),
                   jax.ShapeDtypeStruct((b, last, f), F32),
                   jax.ShapeDtypeStruct((b, last, f), F32)],
        compiler_params=_cparams(("parallel", "arbitrary")),
        name="ffn_up_fused",
    )(xn, w_up, w_up, w_conv, w_conv, cache, cache)


def _levels(c):
    out, h = [], c // 2
    while h >= 1:
        out.append(h)
        h //= 2
    return out


def _prefix_matrix(c):
    blocks = [np.tril(np.ones((c, c), np.float32))]
    for h in _levels(c):
        p = np.zeros((c, c), np.float32)
        for t in range(c):
            mid = (t // (2 * h)) * 2 * h + h
            if t % (2 * h) >= h:
                p[t, mid:t + 1] = 1.0
            else:
                p[t, t + 1:mid] = 1.0
        blocks.append(p)
    return np.concatenate(blocks, axis=0)


def _gated_head_out(o, gate, w):
    o = o * lax.rsqrt(jnp.mean(o * o, axis=-1, keepdims=True) + EPS) * w
    return o * _silu(gate)


def _stack_heads(x, nh, w):
    return jnp.concatenate([x[:, i * w:(i + 1) * w] for i in range(nh)], axis=0)


def _diag_blocks(y, nh, c, w):
    return jnp.concatenate([y[i * c:(i + 1) * c, i * w:(i + 1) * w] for i in range(nh)], axis=0)


def _block_cols(v, nh, c):
    head = lax.broadcasted_iota(jnp.int32, (nh * c, 1), 0) >> (c.bit_length() - 1)
    return jnp.concatenate([jnp.where(head == i, v, 0.0) for i in range(nh)], axis=-1)


def _row_to_col(row):
    n = row.shape[1]
    eye = (lax.broadcasted_iota(jnp.int32, (n, n), 0) == lax.broadcasted_iota(jnp.int32, (n, n), 1))
    return jnp.sum(jnp.where(eye, row, 0.0), axis=1, keepdims=True)


def _each(f, *lists):
    return [f(*args) for args in zip(*lists)]


def _linear_attn_blocks(q, k, v, g, s_cat, p_ref, c, nh, dv):
    dk = q[0].shape[1] // nh
    r = nh * c
    z = [_sel_dot(p_ref[...], g_) for g_ in g]
    b = [z_[0:c] for z_ in z]
    row = lax.broadcasted_iota(jnp.int32, (c, 1), 0)
    ri = lax.broadcasted_iota(jnp.int32, (r, r), 0)
    ci = lax.broadcasted_iota(jnp.int32, (r, r), 1)
    attn = _each(lambda q_, k_: jnp.where(
        ri == ci, jnp.sum(_stack_heads(q_ * k_, nh, dk), axis=-1, keepdims=True), 0.0), q, k)
    for li, h in enumerate(_levels(c)):
        lower = (row & (2 * h - 1)) >= h
        same = (ri >> ((2 * h).bit_length() - 1)) == (ci >> ((2 * h).bit_length() - 1))

        def level(q_, k_, z_):
            e = jnp.exp(z_[(li + 1) * c:(li + 2) * c])
            ql = _stack_heads(jnp.where(lower, q_ * e, 0.0), nh, dk).astype(BF16)
            kl = _stack_heads(jnp.where(lower, 0.0, k_ * e), nh, dk).astype(BF16)
            return jnp.where(same, _dot_nt(ql, kl), 0.0)

        attn = _each(lambda a, q_, k_, z_: a + level(q_, k_, z_), attn, q, k, z)
    vs = [_stack_heads(v_, nh, dv) for v_ in v]
    o = _each(lambda q_, b_, s_, a, vs_: _diag_blocks(
        _mm(_stack_heads(q_ * jnp.exp(b_), nh, dk), s_), nh, c, dv) + _mm(a, vs_), q, b, s_cat, attn, vs)

    def new_state(k_, b_, vs_, s_):
        b_last = b_[c - 1:c]
        kt = _stack_heads(k_ * jnp.exp(b_last - b_), nh, dk).astype(BF16)
        upd = _dot_tn(kt, _block_cols(vs_, nh, c).astype(BF16))
        el = jnp.exp(b_last)
        dec = jnp.concatenate(
            [jnp.broadcast_to(_row_to_col(el[:, i * dk:(i + 1) * dk]), (dk, dv)) for i in range(nh)],
            axis=-1)
        return dec * s_ + upd

    return o, _each(new_state, k, b, vs, s_cat)


def _gla_kernel(q_ref, k_ref, v_ref, gg_ref, sm_ref, wg_ref, bg_ref, nw_ref, p_ref, s0_ref,
                o_ref, s_ref, *, c):
    ci = pl.program_id(1)

    @pl.when(ci == 0)
    def _():
        s_ref[...] = s0_ref[...]

    x = _dot_sel(sm_ref[...], wg_ref[...]) + bg_ref[...]
    g = _log_sigmoid(x) * (1.0 / GLA_GATE_NORM)
    q = q_ref[...] * (GLA_DK ** -0.5)
    gate = gg_ref[...]
    s_cat = jnp.concatenate([s_ref[0, hh] for hh in range(GLA_H)], axis=-1)
    (o,), (s_new,) = _linear_attn_blocks([q], [k_ref[...]], [v_ref[...]], [g], [s_cat], p_ref, c,
                                         GLA_H, GLA_DV)
    for hh in range(GLA_H):
        vs = slice(hh * GLA_DV, (hh + 1) * GLA_DV)
        s_ref[0, hh] = s_new[:, vs]
        o_ref[:, vs] = _gated_head_out(o[hh * c:(hh + 1) * c], gate[:, vs],
                                       nw_ref[...]).astype(o_ref.dtype)


def gla_mixer(proj, small, wg_pad, bg, norm_w, state, b, t, c):
    m = proj.shape[0]
    nc = t // c
    kw = GLA_H * GLA_DK
    vw = GLA_H * GLA_DV
    p = jnp.asarray(_prefix_matrix(c), BF16)
    row = lambda bb, cc: bb * nc + cc
    kern = functools.partial(_gla_kernel, c=c)
    return pl.pallas_call(
        kern,
        grid=(b, nc),
        in_specs=[pl.BlockSpec((c, kw), lambda bb, cc: (row(bb, cc), 0)),
                  pl.BlockSpec((c, kw), lambda bb, cc: (row(bb, cc), 1)),
                  pl.BlockSpec((c, vw), lambda bb, cc: (row(bb, cc), 1)),
                  pl.BlockSpec((c, vw), lambda bb, cc: (row(bb, cc), 2)),
                  pl.BlockSpec((c, LANE), lambda bb, cc: (row(bb, cc), 0)),
                  pl.BlockSpec((LANE, kw), lambda bb, cc: (0, 0)),
                  pl.BlockSpec((1, kw), lambda bb, cc: (0, 0)),
                  pl.BlockSpec((1, GLA_DV), lambda bb, cc: (0, 0)),
                  pl.BlockSpec(p.shape, lambda bb, cc: (0, 0)),
                  pl.BlockSpec((1, GLA_H, GLA_DK, GLA_DV), lambda bb, cc: (bb, 0, 0, 0))],
        out_specs=[pl.BlockSpec((c, vw), lambda bb, cc: (row(bb, cc), 0)),
                   pl.BlockSpec((1, GLA_H, GLA_DK, GLA_DV), lambda bb, cc: (bb, 0, 0, 0))],
        out_shape=[jax.ShapeDtypeStruct((m, vw), BF16),
                   jax.ShapeDtypeStruct((b, GLA_H, GLA_DK, GLA_DV), F32)],
        compiler_params=_cparams(("parallel", "arbitrary")),
        name="gla_mixer",
    )(proj, proj, proj, proj, small, wg_pad, bg.reshape(1, -1), norm_w.reshape(1, -1), p, state)


HG_HB = 4
HG_GS = 2


def _hgrn_kernel(q_ref, f_ref, i_ref, g_ref, lb_ref, nw_ref, p_ref, s0_ref,
                 o_ref, s_ref, *, c):
    ci = pl.program_id(2)

    @pl.when(ci == 0)
    def _():
        s_ref[...] = s0_ref[...]

    lb = lb_ref[...]
    log_lb = jnp.log(jnp.maximum(lb, TINY))
    zf = f_ref[...]
    a = log_lb
    cc = jnp.log1p(-lb) + _log_sigmoid(zf)
    log_f = jnp.maximum(a, cc) + jnp.log1p(jnp.exp(-jnp.abs(a - cc)))
    key = (1.0 - lb) * jax.nn.sigmoid(-zf)
    q = _silu(q_ref[...])
    v = i_ref[...]
    gate = g_ref[...]
    gw = HG_HB * HG_DK
    groups = [slice(gi * gw, (gi + 1) * gw) for gi in range(HG_GS)]
    s_cat = [jnp.concatenate([s_ref[0, gi * HG_HB + hh] for hh in range(HG_HB)], axis=-1)
             for gi in range(HG_GS)]
    o, s_new = _linear_attn_blocks([q[:, gs] for gs in groups], [key[:, gs] for gs in groups],
                                   [v[:, gs] for gs in groups], [log_f[:, gs] for gs in groups],
                                   s_cat, p_ref, c, HG_HB, HG_DV)
    for gi in range(HG_GS):
        for hh in range(HG_HB):
            sl = slice(hh * HG_DV, (hh + 1) * HG_DV)
            ol = slice(gi * gw + hh * HG_DV, gi * gw + (hh + 1) * HG_DV)
            s_ref[0, gi * HG_HB + hh] = s_new[gi][:, sl]
            o_ref[:, ol] = _gated_head_out(o[gi][hh * c:(hh + 1) * c], gate[:, ol],
                                           nw_ref[...]).astype(o_ref.dtype)


def hgrn_mixer(proj, lb, norm_w, state, b, t, c):
    col0 = 0
    m = proj.shape[0]
    nc = t // c
    hs = HG_HB * HG_GS
    ng = HG_H // hs
    w = hs * HG_DK
    p = jnp.asarray(_prefix_matrix(c), BF16)
    row = lambda bb, g, cc: bb * nc + cc
    kern = functools.partial(_hgrn_kernel, c=c)
    return pl.pallas_call(
        kern,
        grid=(b, ng, nc),
        in_specs=[pl.BlockSpec((c, w), lambda bb, g, cc: (row(bb, g, cc), col0 + g)),
                  pl.BlockSpec((c, w), lambda bb, g, cc: (row(bb, g, cc), col0 + ng + g)),
                  pl.BlockSpec((c, w), lambda bb, g, cc: (row(bb, g, cc), col0 + 2 * ng + g)),
                  pl.BlockSpec((c, w), lambda bb, g, cc: (row(bb, g, cc), col0 + 3 * ng + g)),
                  pl.BlockSpec((1, w), lambda bb, g, cc: (0, g)),
                  pl.BlockSpec((1, HG_DV), lambda bb, g, cc: (0, 0)),
                  pl.BlockSpec(p.shape, lambda bb, g, cc: (0, 0)),
                  pl.BlockSpec((1, hs, HG_DK, HG_DV), lambda bb, g, cc: (bb, g, 0, 0))],
        out_specs=[pl.BlockSpec((c, w), lambda bb, g, cc: (row(bb, g, cc), g)),
                   pl.BlockSpec((1, hs, HG_DK, HG_DV), lambda bb, g, cc: (bb, g, 0, 0))],
        out_shape=[jax.ShapeDtypeStruct((m, HG_H * HG_DV), BF16),
                   jax.ShapeDtypeStruct((b, HG_H, HG_DK, HG_DV), F32)],
        compiler_params=_cparams(("parallel", "parallel", "arbitrary")),
        name="hgrn_mixer",
    )(proj, proj, proj, proj, lb.reshape(1, -1), norm_w.reshape(1, -1), p, state)


GDN_HB = 4
GDN_GS = 4


def _unit_lower_inverse(ns, r, c):
    ri = lax.broadcasted_iota(jnp.int32, (r, r), 0)
    ci = lax.broadcasted_iota(jnp.int32, (r, r), 1)
    eye = (ri == ci).astype(F32)
    d = [jnp.where((ri >> 4) == (ci >> 4), n, 0.0) for n in ns]
    x = [eye - d_ for d_ in d]
    p = d
    for _ in range(3):
        p = _each(_mm, p, p)
        x = _each(lambda x_, p_: x_ + _mm(x_, p_), x, p)
    size = 16
    while size < c:
        sh = size.bit_length() - 1
        mask = ((ri >> (sh + 1)) == (ci >> (sh + 1))) & ((ri >> sh) > (ci >> sh))
        lx = _each(lambda n, x_: _mm(jnp.where(mask, n, 0.0), x_), ns, x)
        x = _each(lambda x_, lx_: x_ - _mm(x_, lx_), x, lx)
        size *= 2
    return x


def _gdn_blocks(q, k, v, beta, b_c, b_last, s_cat, c, nh):
    r = nh * c
    ri = lax.broadcasted_iota(jnp.int32, (r, r), 0)
    ci = lax.broadcasted_iota(jnp.int32, (r, r), 1)
    shc = c.bit_length() - 1
    causal = ((ri >> shc) == (ci >> shc)) & (ri >= ci)

    def decay_of(bc):
        b_r = jnp.sum(jnp.where(ri == ci, bc, 0.0), axis=0, keepdims=True)
        return jnp.where(causal, jnp.exp(jnp.minimum(bc - b_r, 0.0)), 0.0)

    decay = _each(decay_of, b_c)
    kb = [k_.astype(BF16) for k_ in k]
    kq = _each(lambda kb_, q_: _dot_nt(jnp.concatenate([kb_, q_.astype(BF16)], axis=0), kb_), kb, q)
    n = _each(lambda be, kq_, de: jnp.where(ri > ci, be * kq_[:r] * de, 0.0), beta, kq, decay)
    tinv = _unit_lower_inverse(n, r, c)
    eb = [jnp.exp(bc) for bc in b_c]
    sol = _each(lambda t_, v_, k_, be, e_: _mm(t_, jnp.concatenate([v_ * be, k_ * (be * e_)], axis=-1)),
                tinv, v, k, beta, eb)
    ws_qs = _each(lambda so, q_, e_, s_: _mm(jnp.concatenate([so[:, GDN_DV:], q_ * e_], axis=0), s_),
                  sol, q, eb, s_cat)
    v_new = _each(lambda so, wq: so[:, :GDN_DV] - _diag_blocks(wq[:r], nh, c, GDN_DV), sol, ws_qs)
    o = _each(lambda wq, kq_, de, vn: _diag_blocks(wq[r:], nh, c, GDN_DV) + _mm(kq_[r:] * de, vn),
              ws_qs, kq, decay, v_new)
    upd = _each(lambda k_, bl, bc, vn: _dot_tn((k_ * jnp.exp(bl - bc)).astype(BF16),
                                               _block_cols(vn, nh, c).astype(BF16)),
                k, b_last, b_c, v_new)

    def decayed(bl, s_, up):
        el = jnp.exp(bl)
        dec = jnp.concatenate(
            [jnp.broadcast_to(el[i * c:i * c + 1], (1, GDN_DV)) for i in range(nh)], axis=-1)
        return dec * s_ + up

    return o, _each(decayed, b_last, s_cat, upd)


def _gdn_kernel(q_ref, k_ref, v_ref, z_ref, gc_ref, wq_ref, wk_ref, wv_ref,
                pc_ref, nw_ref, tril_ref, s0_ref, cq0_ref, ck0_ref, cv0_ref,
                o_ref, s_ref, cq_ref, ck_ref, cv_ref,
                xq, xk, xv, *, c, nc):
    ci = pl.program_id(2)
    last = GDN_CONV - 1

    @pl.when(ci == 0)
    def _():
        s_ref[...] = s0_ref[...]

    def conv(x_ref, w_ref, c0_ref, c_ref, scr):
        @pl.when(ci == 0)
        def _():
            scr[8 - last:8, :] = c0_ref[0]

        scr[8:8 + c, :] = x_ref[...]
        y = scr[8:8 + c, :] * w_ref[last:last + 1, :]
        for j in range(last):
            y = y + scr[8 - last + j:8 - last + j + c, :] * w_ref[j:j + 1, :]
        tail = scr[8 + c - last:8 + c, :]
        scr[8 - last:8, :] = tail

        @pl.when(ci == nc - 1)
        def _():
            c_ref[0] = tail

        return _silu(y)

    qa = conv(q_ref, wq_ref, cq0_ref, cq_ref, xq)
    ka = conv(k_ref, wk_ref, ck0_ref, ck_ref, xk)
    va = conv(v_ref, wv_ref, cv0_ref, cv_ref, xv)

    nh = GDN_HB
    gw = nh * GDN_DK
    q, k, v, beta, b_c, b_last, s_cat = [], [], [], [], [], [], []
    for gi in range(GDN_GS):
        gcol = gc_ref[gi]
        pc = pc_ref[gi]
        beta_all = jax.nn.sigmoid(gcol)
        g_c = pc[0:1, :] * _softplus(gcol + pc[1:2, :])
        b_col = _sel_dot(tril_ref[...], g_c)
        beta.append(jnp.concatenate([beta_all[:, hh:hh + 1] for hh in range(nh)], axis=0))
        b_c.append(jnp.concatenate([b_col[:, nh + hh:nh + hh + 1] for hh in range(nh)], axis=0))
        b_last.append(jnp.concatenate(
            [jnp.broadcast_to(b_col[c - 1:c, nh + hh:nh + hh + 1], (c, 1)) for hh in range(nh)],
            axis=0))
        gs = slice(gi * gw, (gi + 1) * gw)
        qg = _stack_heads(qa[:, gs], nh, GDN_DK)
        kg = _stack_heads(ka[:, gs], nh, GDN_DK)
        q.append(qg * lax.rsqrt(jnp.sum(qg * qg, axis=-1, keepdims=True) + EPS) * (GDN_DK ** -0.5))
        k.append(kg * lax.rsqrt(jnp.sum(kg * kg, axis=-1, keepdims=True) + EPS))
        v.append(_stack_heads(va[:, gs], nh, GDN_DV))
        s_cat.append(jnp.concatenate([s_ref[0, gi * nh + hh] for hh in range(nh)], axis=-1))
    o, s_new = _gdn_blocks(q, k, v, beta, b_c, b_last, s_cat, c, nh)
    z = z_ref[...]
    for gi in range(GDN_GS):
        for hh in range(nh):
            sl = slice(hh * GDN_DV, (hh + 1) * GDN_DV)
            zl = slice(gi * gw + hh * GDN_DV, gi * gw + (hh + 1) * GDN_DV)
            s_ref[0, gi * nh + hh] = s_new[gi][:, sl]
            o_ref[:, zl] = _gated_head_out(o[gi][hh * c:(hh + 1) * c], z[:, zl],
                                           nw_ref[...]).astype(o_ref.dtype)


def gdn_mixer(proj, gates_c, conv_w, par_c, norm_w, state, cache, b, t, c):
    m = proj.shape[0]
    nc = t // c
    hs = GDN_HB * GDN_GS
    ng = GDN_H // hs
    w = hs * GDN_DK
    col0 = 0
    last = GDN_CONV - 1
    tril = jnp.asarray(np.tril(np.ones((c, c), np.float32)), BF16)
    row = lambda bb, g, cc: bb * nc + cc
    kern = functools.partial(_gdn_kernel, c=c, nc=nc)
    big = lambda off: pl.BlockSpec((c, w), lambda bb, g, cc: (row(bb, g, cc), col0 + off * ng + g))
    cw = lambda off: pl.BlockSpec((GDN_CONV, w), lambda bb, g, cc: (0, off * ng + g))
    c0 = lambda off: pl.BlockSpec((1, last, w), lambda bb, g, cc: (bb, 0, off * ng + g))
    cout = pl.BlockSpec((1, last, w), lambda bb, g, cc: (bb, 0, g))
    outs = pl.pallas_call(
        kern,
        grid=(b, ng, nc),
        in_specs=[big(0), big(1), big(2), big(3),
                  pl.BlockSpec((GDN_GS, c, LANE), lambda bb, g, cc: (g, row(bb, g, cc), 0)),
                  cw(0), cw(1), cw(2),
                  pl.BlockSpec((GDN_GS, 2, LANE), lambda bb, g, cc: (g, 0, 0)),
                  pl.BlockSpec((1, GDN_DV), lambda bb, g, cc: (0, 0)),
                  pl.BlockSpec((c, c), lambda bb, g, cc: (0, 0)),
                  pl.BlockSpec((1, hs, GDN_DK, GDN_DV), lambda bb, g, cc: (bb, g, 0, 0)),
                  c0(0), c0(1), c0(2)],
        out_specs=[pl.BlockSpec((c, w), lambda bb, g, cc: (row(bb, g, cc), g)),
                   pl.BlockSpec((1, hs, GDN_DK, GDN_DV), lambda bb, g, cc: (bb, g, 0, 0)),
                   cout, cout, cout],
        out_shape=[jax.ShapeDtypeStruct((m, GDN_H * GDN_DV), BF16),
                   jax.ShapeDtypeStruct((b, GDN_H, GDN_DK, GDN_DV), F32),
                   jax.ShapeDtypeStruct((b, last, GDN_H * GDN_DK), F32),
                   jax.ShapeDtypeStruct((b, last, GDN_H * GDN_DK), F32),
                   jax.ShapeDtypeStruct((b, last, GDN_H * GDN_DV), F32)],
        scratch_shapes=[pltpu.VMEM((8 + c, w), F32), pltpu.VMEM((8 + c, w), F32),
                        pltpu.VMEM((8 + c, w), F32)],
        compiler_params=_cparams(("parallel", "parallel", "arbitrary")),
        name="gdn_mixer",
    )(proj, proj, proj, proj, gates_c, conv_w, conv_w, conv_w, par_c,
      norm_w.reshape(1, -1), tril, state, cache, cache, cache)
    o, s, cq, ck, cv = outs
    return o, s, jnp.concatenate([cq, ck, cv], axis=-1)


def _row_tile(m, cap):
    tm = min(m, cap)
    while m % tm:
        tm //= 2
    return tm


def _group_gates(small):
    m = small.shape[0]
    ng = GDN_H // GDN_HB
    db = small[:, GLA_RANK:GLA_RANK + GDN_H].reshape(m, ng, GDN_HB)
    da = small[:, GLA_RANK + GDN_H:GLA_RANK + 2 * GDN_H].reshape(m, ng, GDN_HB)
    g = jnp.concatenate([db, da, jnp.zeros((m, ng, LANE - 2 * GDN_HB), F32)], axis=-1)
    return jnp.transpose(g, (1, 0, 2))


def _group_params(a_log, dt_bias):
    ng = GDN_H // GDN_HB
    na = -jnp.exp(a_log.astype(F32)).reshape(ng, GDN_HB)
    dt = dt_bias.astype(F32).reshape(ng, GDN_HB)
    pad = jnp.zeros((ng, GDN_HB), F32)
    tail = jnp.zeros((ng, LANE - 2 * GDN_HB), F32)
    return jnp.stack([jnp.concatenate([pad, na, tail], axis=-1),
                      jnp.concatenate([pad, dt, tail], axis=-1)], axis=1)


COL_GLA = 0
N_GLA = 2 * GLA_H * GLA_DK + 2 * GLA_H * GLA_DV
COL_LR = COL_GLA + N_GLA
COL_GDN = COL_LR + GLA_RANK
N_GDN = 2 * GDN_H * GDN_DK + 2 * GDN_H * GDN_DV
COL_BA = COL_GDN + N_GDN
COL_HG = COL_BA + 2 * GDN_H
N_HG = 2 * HG_H * HG_DK + 2 * HG_H * HG_DV

COL_TILE = 256
DOWN_TILES = (1024, 512)


def _prep_weights(w_gla_gate, w_down):
    wg_pad = jnp.pad(w_gla_gate.astype(BF16), ((0, 0), (0, LANE - GLA_RANK), (0, 0)))
    return wg_pad, w_down.astype(BF16)


def _trunk(x, pe, st_gla, st_gdn, cb_gdn, st_hg, cb_ffn, wts, prm, lower_bounds, norm_final):
    (w_in_t, wg_pad, w_out, w_up, w_down, w_pg, w_pp) = wts
    (norm_mix, b_gla_gate, gla_norm, w_gdn_conv, gdn_a_log, gdn_dt_bias, gdn_norm, hgrn_norm,
     norm_ffn, w_ffn_conv, norm_ple) = prm
    b, t, d = x.shape
    depth = w_in_t.shape[0]
    m = b * t
    c = min(MAX_CHUNK, t)
    tm = _row_tile(m, ROW_TILE)
    tr = _row_tile(t, 256)
    tn = COL_TILE
    f = w_down.shape[1]
    h = x.reshape(m, d)
    n_gla, n_gdn, n_gconv, n_hg, n_fconv = [], [], [], [], []
    for li in range(depth):
        xn = rmsnorm(h, norm_mix[li], BF16, _row_tile(m, 256))
        p_gla = matmul_t(xn, w_in_t, li, F32, tm, tn, COL_GLA, N_GLA)
        p_gdn = matmul_t(xn, w_in_t, li, F32, tm, tn, COL_GDN, N_GDN)
        p_hg = matmul_t(xn, w_in_t, li, F32, tm, tn, COL_HG, N_HG)
        small = gate_proj(xn, w_in_t, li, _row_tile(m, 1024), COL_LR, GLA_RANK, COL_BA, 2 * GDN_H)

        o_gla, s_gla = gla_mixer(p_gla, small, wg_pad[li], b_gla_gate[li], gla_norm[li],
                                 st_gla[li], b, t, c)
        o_gdn, s_gdn, b_gdn = gdn_mixer(p_gdn, _group_gates(small), w_gdn_conv[li],
                                        _group_params(gdn_a_log[li], gdn_dt_bias[li]), gdn_norm[li],
                                        st_gdn[li], cb_gdn[li], b, t, c)
        o_hg, s_hg = hgrn_mixer(p_hg, lower_bounds[li], hgrn_norm[li], st_hg[li], b, t, c)
        mix = jnp.concatenate([o_gla, o_gdn, o_hg], axis=-1)
        h = matmul_residual(mix, w_out, li, h, tm, tn, mix.shape[1])

        xn = rmsnorm(h, norm_ffn[li], BF16, _row_tile(m, 256))
        if t == tm:
            hid, cg, cv = ffn_up_fused(xn, w_up, li, w_ffn_conv[li], cb_ffn[li], tn)
        else:
            up = matmul(xn, w_up, li, F32, tm, tn)
            hid, cg, cv = ffn_conv_act(up, w_ffn_conv[li], cb_ffn[li], b, t, tr, f // 2)
        b_ffn = jnp.concatenate([cg, cv], axis=-1)
        h = matmul_residual(hid, w_down, li, h, _row_tile(m, DOWN_TILES[0]), DOWN_TILES[1], f // 2)

        xn = rmsnorm(h, norm_ple[li], BF16, _row_tile(m, 256))
        h = ple_update(xn, w_pg, pe[li].reshape(m, -1).astype(BF16), w_pp, li, h, tm, tn)

        n_gla.append(s_gla)
        n_gdn.append(s_gdn)
        n_gconv.append(b_gdn)
        n_hg.append(s_hg)
        n_fconv.append(b_ffn)
    y = rmsnorm(h, norm_final, F32, _row_tile(m, 256)).reshape(b, t, d)
    return (y, jnp.stack(n_gla), jnp.stack(n_gdn), jnp.stack(n_gconv),
            jnp.stack(n_hg), jnp.stack(n_fconv))


def kernel(x_prompt, x_sample, p_prompt, p_sample, state_gla, state_gdn, cache_gdn_conv, state_hgrn, cache_ffn_conv, norm_mix, w_in, w_gla_gate, b_gla_gate, gla_norm, w_gdn_conv, gdn_a_log, gdn_dt_bias, gdn_norm, hgrn_lb, hgrn_norm, w_out, norm_ffn, w_up, w_ffn_conv, w_down, norm_ple, w_ple_gate, w_ple_proj, norm_final):
    depth = w_in.shape[0]
    bp = x_prompt.shape[0]
    wg_pad, w_down_b = _prep_weights(w_gla_gate, w_down)
    wts = (jnp.swapaxes(w_in, 1, 2), wg_pad, w_out, w_up, w_down_b, w_ple_gate, w_ple_proj)
    prm = (norm_mix, b_gla_gate, gla_norm, w_gdn_conv, gdn_a_log, gdn_dt_bias, gdn_norm,
           hgrn_norm, norm_ffn, w_ffn_conv, norm_ple)
    sm = jax.nn.softmax(hgrn_lb.astype(F32), axis=0)
    lower_bounds = jnp.cumsum(sm, axis=0) - sm[0]
    zeros = lambda a: jnp.zeros((depth, bp) + a.shape[2:], F32)
    prompt = _trunk(x_prompt, p_prompt, zeros(state_gla), zeros(state_gdn), zeros(cache_gdn_conv),
                    zeros(state_hgrn), zeros(cache_ffn_conv), wts, prm, lower_bounds, norm_final)
    sample = _trunk(x_sample, p_sample, state_gla, state_gdn, cache_gdn_conv, state_hgrn,
                    cache_ffn_conv, wts, prm, lower_bounds, norm_final)
    return (prompt[0], sample[0]) + prompt[1:] + sample[1:]
```
